```python
import math
import jax, jax.numpy as jnp
from jax import lax
import numpy as np

D_MODEL = 1024
BATCH = 16
SEQ = 2048
DEPTH = 2
DEC_BATCH = 32
DEC_SEQ = 1
PAST_LEN = 16384
PAGE_SIZE = 128

HS_A = 64
H_A = D_MODEL // HS_A
D_A = H_A * HS_A
R_W = 64
R_A = 64
LN_X_EPS = 64e-5
HD_B = 64
H_B = D_MODEL // (2 * HD_B)
VD_B = 2 * HD_B
QK_B = H_B * 2 * HD_B
D_B = H_B * VD_B
Q_BLOCK = 128
SUBLN_EPS = 1e-5
D_C = D_MODEL
NB_C = 4
BS_C = D_C // NB_C
CONV_W = 4
LRU_C = 8.0
N_MEM = 256
H_MEM = 4
HD_MEM = D_MODEL // H_MEM
D_MEM = H_MEM * HD_MEM
N_BRANCH = 3
W_SHIFT = 3 * D_A + R_W + R_A
IN_SPLITS = (W_SHIFT, D_A, QK_B, QK_B, D_B, D_B, D_C, D_C, N_BRANCH * D_MODEL)
N_IN = sum(IN_SPLITS)
RMS_EPS = 1e-6

kernel_name = "hybrid_rwkv7_diffattn_rglru_decode_step"


def _rms(x, g, eps=RMS_EPS):
    xf = x.astype(jnp.float32)
    y = xf * lax.rsqrt(jnp.mean(xf * xf, axis=-1, keepdims=True) + eps) * g.astype(jnp.float32)
    return y.astype(x.dtype)


def _split_in(p):
    return jnp.split(p, np.cumsum(IN_SPLITS)[:-1], axis=-1)


def _heads(t, h, d):
    return t.reshape(t.shape[:-1] + (h, d))


def _rwkv_scan(S0, r, decay, k, v, aa, bb):
    xs = tuple(jnp.moveaxis(t, 1, 0) for t in (r, decay, k, v, aa, bb))

    def step(S, inp):
        rt, wt, kt, vt, at, bt = inp
        S = (S * wt[:, :, None, :]
             + jnp.einsum('bhij,bhj->bhi', S, at)[..., None] * bt[:, :, None, :]
             + vt[..., None] * kt[:, :, None, :])
        return S, jnp.einsum('bhij,bhj->bhi', S, rt)

    S_T, ys = lax.scan(step, S0, xs)
    return jnp.moveaxis(ys, 0, 1), S_T


def _rwkv_branch(W, l, pa, g_a, shift_prev, S0):
    B, T, _ = pa.shape
    f32 = jnp.float32
    prev = jnp.concatenate([shift_prev[:, None, :].astype(pa.dtype), pa[:, :-1]], axis=1)
    pm = (pa + W['rwkv_mu'][l] * (prev - pa)).astype(f32)
    r, k, v, wl, al = jnp.split(pm, np.cumsum((D_A, D_A, D_A, R_W)), axis=-1)
    w = -jax.nn.softplus(-(W['rwkv_w0'][l] + jnp.tanh(wl) @ W['rwkv_w2'][l])) - 0.5
    decay = jnp.exp(-jnp.exp(w))
    a = jax.nn.sigmoid(W['rwkv_a0'][l] + al @ W['rwkv_a2'][l])
    kk = _heads(k * W['rwkv_kk'][l], H_A, HS_A)
    kk = kk / jnp.maximum(jnp.sqrt(jnp.sum(kk * kk, axis=-1, keepdims=True)), 1e-12)
    k = k * (1.0 + (a - 1.0) * W['rwkv_ka'][l])
    r, k, v, a, decay = (_heads(t, H_A, HS_A) for t in (r, k, v, a, decay))
    y, S_T = _rwkv_scan(S0.astype(f32), r, decay, k, v, -kk, kk * a)
    mean = jnp.mean(y, axis=-1, keepdims=True)
    var = jnp.mean(jnp.square(y - mean), axis=-1, keepdims=True)
    yn = ((y - mean) * lax.rsqrt(var + LN_X_EPS)).reshape(B, T, D_A)
    yn = yn * W['rwkv_ln_g'][l] + W['rwkv_ln_b'][l]
    bonus = (jnp.sum(r * k * W['rwkv_rk'][l], axis=-1, keepdims=True) * v).reshape(B, T, D_A)
    out = (yn + bonus).astype(pa.dtype) * jax.nn.silu(g_a)
    return out, pa[:, -1], S_T


def _diff_attn_prompt(q, k, v, lam):
    B, T = q.shape[:2]
    scale = HD_B ** -0.5
    kpos = jnp.arange(T)

    def block(i):
        qb = lax.dynamic_slice_in_dim(q, i * Q_BLOCK, Q_BLOCK, axis=1)
        s = jnp.einsum('bqhcd,bkhcd->bhcqk', qb, k).astype(jnp.float32) * scale
        qpos = i * Q_BLOCK + jnp.arange(Q_BLOCK)
        s = jnp.where(qpos[:, None] >= kpos[None, :], s, -jnp.inf)
        p = jax.nn.softmax(s, axis=-1)
        wgt = p[:, :, 0] - lam * p[:, :, 1]
        return jnp.einsum('bhqk,bkhd->bqhd', wgt.astype(v.dtype), v)

    o = lax.map(block, jnp.arange(T // Q_BLOCK))
    return jnp.moveaxis(o, 0, 1).reshape(B, T, H_B, VD_B)


def _diff_attn_sample(q, k_new, v_new, lam, cache_k, cache_v, page_table, layer):
    Bd, S = q.shape[:2]
    n_pages = page_table.shape[1]
    past = n_pages * PAGE_SIZE
    scale = HD_B ** -0.5

    def page_scores(p):
        kp = cache_k[layer, page_table[:, p]].reshape(Bd, PAGE_SIZE, H_B, 2, HD_B)
        return jnp.einsum('bqhcd,bkhcd->bhcqk', q, kp.astype(q.dtype)).astype(jnp.float32)

    s_past = lax.map(page_scores, jnp.arange(n_pages))
    s_past = jnp.moveaxis(s_past, 0, 4).reshape(Bd, H_B, 2, S, past)
    s_new = jnp.einsum('bqhcd,bkhcd->bhcqk', q, k_new).astype(jnp.float32)
    causal = jnp.arange(S)[:, None] >= jnp.arange(S)[None, :]
    s_new = jnp.where(causal, s_new, -jnp.inf)
    s = jnp.concatenate([s_past, s_new], axis=-1) * scale
    p = jax.nn.softmax(s, axis=-1)
    wgt = p[:, :, 0] - lam * p[:, :, 1]
    w_past = jnp.moveaxis(wgt[..., :past].reshape(Bd, H_B, S, n_pages, PAGE_SIZE), 3, 0)

    def page_values(acc, xs):
        pidx, wp = xs
        vp = cache_v[layer, page_table[:, pidx]].astype(jnp.float32)
        return acc + jnp.einsum('bhqk,bkhd->bqhd', wp, vp), None

    acc0 = jnp.zeros((Bd, S, H_B, VD_B), jnp.float32)
    o, _ = lax.scan(page_values, acc0, (jnp.arange(n_pages), w_past))
    o = o + jnp.einsum('bhqk,bkhd->bqhd', wgt[..., past:], v_new.astype(jnp.float32))
    return o.astype(q.dtype)


def _linear_recurrence(a, b, h0):
    def combine(c1, c2):
        a1, b1 = c1
        a2, b2 = c2
        return a1 * a2, a2 * b1 + b2
    a_cum, b_cum = lax.associative_scan(combine, (a, b), axis=1)
    return a_cum * h0[:, None, :] + b_cum


def _lru_branch(W, l, xc, g_c, conv_prev, h0):
    B, T, _ = xc.shape
    f32 = jnp.float32
    full = jnp.concatenate([conv_prev.astype(xc.dtype), xc], axis=1)
    cw = W['lru_conv_w'][l]
    xconv = sum(full[:, j:j + T] * cw[j] for j in range(CONV_W)) + W['lru_conv_b'][l]
    xconv = xconv.astype(f32)
    xb = xconv.reshape(B, T, NB_C, BS_C)
    gate_r = jax.nn.sigmoid(jnp.einsum('btni,nij->btnj', xb, W['lru_wa'][l]).reshape(B, T, D_C) + W['lru_ba'][l])
    gate_i = jax.nn.sigmoid(jnp.einsum('btni,nij->btnj', xb, W['lru_wx'][l]).reshape(B, T, D_C) + W['lru_bx'][l])
    log_a = -LRU_C * gate_r * jax.nn.softplus(-W['lru_lambda'][l].astype(f32))
    a = jnp.exp(log_a)
    b = jnp.sqrt(-jnp.expm1(2.0 * log_a)) * (gate_i * xconv)
    h = _linear_recurrence(a, b, h0.astype(f32))
    out = h.astype(xc.dtype) * jax.nn.silu(g_c)
    return out, full[:, -(CONV_W - 1):], h[:, -1]


def _mem_kv(W, l, mem):
    B = mem.shape[0]
    kv = _rms(mem, W['norm_memkv'][l]) @ W['w_mem_kv'][l]
    mk, mv = jnp.split(kv, 2, axis=-1)
    return mk.reshape(B, N_MEM, H_MEM, HD_MEM), mv.reshape(B, N_MEM, H_MEM, HD_MEM)


def _mem_attn(xn, mk, mv, w_q, w_o):
    B, T, _ = xn.shape
    q = (xn @ w_q).reshape(B, T, H_MEM, HD_MEM)
    s = jnp.einsum('bthd,bmhd->bhtm', q, mk.astype(q.dtype)).astype(jnp.float32) * HD_MEM ** -0.5
    p = jax.nn.softmax(s, axis=-1).astype(q.dtype)
    o = jnp.einsum('bhtm,bmhd->bthd', p, mv.astype(q.dtype)).reshape(B, T, D_MEM)
    return o @ w_o


def _layer(W, l, x, shift_prev, S0, conv_prev, h0, mem_k, mem_v, attn_fn):
    B, T, _ = x.shape
    xn = _rms(x, W['norm_mix'][l])
    pa, g_a, qb, kb, vb, g_b, xc, g_c, gts = _split_in(xn @ W['w_in'][l])
    out_a, shift_new, S_new = _rwkv_branch(W, l, pa, g_a, shift_prev, S0)
    q = qb.reshape(B, T, H_B, 2, HD_B)
    k = kb.reshape(B, T, H_B, 2, HD_B)
    v = vb.reshape(B, T, H_B, VD_B)
    lam_init = 0.8 - 0.6 * math.exp(-0.3 * l)
    lam = (jnp.exp(jnp.sum(W['diff_lq1'][l].astype(jnp.float32) * W['diff_lk1'][l]))
           - jnp.exp(jnp.sum(W['diff_lq2'][l].astype(jnp.float32) * W['diff_lk2'][l])) + lam_init)
    o = attn_fn(q, k, v, lam)
    o = _rms(o, W['diff_subln'][l], SUBLN_EPS) * (1.0 - lam_init)
    out_b = o.reshape(B, T, D_B) * jax.nn.silu(g_b)
    out_c, conv_new, h_new = _lru_branch(W, l, xc, g_c, conv_prev, h0)
    wb = W['w_branch'][l]
    gs = jax.nn.sigmoid(gts.reshape(B, T, N_BRANCH, D_MODEL).astype(jnp.float32)).astype(x.dtype)
    merged = gs[:, :, 0] * (out_a @ wb[0]) + gs[:, :, 1] * (out_b @ wb[1]) + gs[:, :, 2] * (out_c @ wb[2])
    x = x + merged @ W['w_out'][l]
    x = x + _mem_attn(_rms(x, W['norm_mem'][l]), mem_k, mem_v, W['w_mem_q'][l], W['w_mem_o'][l])
    return x, k.reshape(B, T, H_B, 2 * HD_B), v, S_new, shift_new, h_new, conv_new


def setup_inputs(seed: int = 0) -> dict:
    key = jax.random.key(seed)
    ks = list(jax.random.split(key, 64))
    f32 = jnp.float32

    def nrm(shape, scale=1.0):
        return jax.random.normal(ks.pop(), shape, f32) * scale

    def gain(shape):
        return 1.0 + nrm(shape, 0.05)

    n_pages = PAST_LEN // PAGE_SIZE
    n_used = DEC_BATCH * n_pages
    n_pool = n_used + max(1, n_used // 4)
    page_table = jax.random.permutation(ks.pop(), n_pool)[:n_used].reshape(DEC_BATCH, n_pages).astype(jnp.int32)
    u = jax.random.uniform(ks.pop(), (DEPTH, D_C), f32, 0.9, 0.999)
    a_lru = u ** (1.0 / LRU_C)
    lru_lambda = jnp.log(a_lru) - jnp.log1p(-a_lru)
    return {
        'x_prompt': nrm((BATCH, SEQ, D_MODEL)),
        'x_sample': nrm((DEC_BATCH, DEC_SEQ, D_MODEL)),
        'cache_diff_k': nrm((DEPTH, n_pool, PAGE_SIZE, H_B, 2 * HD_B)),
        'cache_diff_v': nrm((DEPTH, n_pool, PAGE_SIZE, H_B, VD_B)),
        'cache_mem_k': nrm((DEPTH, DEC_BATCH, N_MEM, H_MEM, HD_MEM)),
        'cache_mem_v': nrm((DEPTH, DEC_BATCH, N_MEM, H_MEM, HD_MEM)),
        'state_rwkv_S': nrm((DEPTH, DEC_BATCH, H_A, HS_A, HS_A), 0.3),
        'state_rwkv_shift': nrm((DEPTH, DEC_BATCH, W_SHIFT)),
        'state_lru_h': nrm((DEPTH, DEC_BATCH, D_C), 0.5),
        'state_lru_conv': nrm((DEPTH, DEC_BATCH, CONV_W - 1, D_C)),
        'page_table': page_table,
        'mem_prompt': nrm((BATCH, N_MEM, D_MODEL)),
        'norm_mix': gain((DEPTH, D_MODEL)),
        'w_in': nrm((DEPTH, D_MODEL, N_IN), D_MODEL ** -0.5),
        'rwkv_mu': jax.random.uniform(ks.pop(), (DEPTH, W_SHIFT), f32),
        'rwkv_w0': jax.random.uniform(ks.pop(), (DEPTH, D_A), f32, -5.0, 1.0),
        'rwkv_w2': nrm((DEPTH, R_W, D_A), 0.5 * R_W ** -0.5),
        'rwkv_a0': nrm((DEPTH, D_A), 0.1),
        'rwkv_a2': nrm((DEPTH, R_A, D_A), 0.5 * R_A ** -0.5),
        'rwkv_kk': 0.85 + nrm((DEPTH, D_A), 0.05),
        'rwkv_ka': gain((DEPTH, D_A)),
        'rwkv_rk': nrm((DEPTH, H_A, HS_A), 0.1),
        'rwkv_ln_g': gain((DEPTH, D_A)),
        'rwkv_ln_b': nrm((DEPTH, D_A), 0.02),
        'diff_lq1': nrm((DEPTH, HD_B), 0.1),
        'diff_lk1': nrm((DEPTH, HD_B), 0.1),
        'diff_lq2': nrm((DEPTH, HD_B), 0.1),
        'diff_lk2': nrm((DEPTH, HD_B), 0.1),
        'diff_subln': gain((DEPTH, VD_B)),
        'lru_conv_w': nrm((DEPTH, CONV_W, D_C), 0.5),
        'lru_conv_b': nrm((DEPTH, D_C), 0.02),
        'lru_wa': nrm((DEPTH, NB_C, BS_C, BS_C), BS_C ** -0.5),
        'lru_ba': nrm((DEPTH, D_C), 0.02),
        'lru_wx': nrm((DEPTH, NB_C, BS_C, BS_C), BS_C ** -0.5),
        'lru_bx': nrm((DEPTH, D_C), 0.02),
        'lru_lambda': lru_lambda,
        'w_branch': nrm((DEPTH, N_BRANCH, D_A, D_MODEL), D_A ** -0.5),
        'w_out': nrm((DEPTH, D_MODEL, D_MODEL), D_MODEL ** -0.5),
        'norm_mem': gain((DEPTH, D_MODEL)),
        'norm_memkv': gain((DEPTH, D_MODEL)),
        'w_mem_q': nrm((DEPTH, D_MODEL, D_MEM), D_MODEL ** -0.5),
        'w_mem_kv': nrm((DEPTH, D_MODEL, 2 * D_MEM), D_MODEL ** -0.5),
        'w_mem_o': nrm((DEPTH, D_MEM, D_MODEL), D_MEM ** -0.5),
        'norm_final': gain((D_MODEL,)),
    }


def reference(x_prompt, x_sample, cache_diff_k, cache_diff_v, cache_mem_k, cache_mem_v,
              state_rwkv_S, state_rwkv_shift, state_lru_h, state_lru_conv, page_table, mem_prompt,
              norm_mix, w_in, rwkv_mu, rwkv_w0, rwkv_w2, rwkv_a0, rwkv_a2, rwkv_kk, rwkv_ka, rwkv_rk,
              rwkv_ln_g, rwkv_ln_b, diff_lq1, diff_lk1, diff_lq2, diff_lk2, diff_subln,
              lru_conv_w, lru_conv_b, lru_wa, lru_ba, lru_wx, lru_bx, lru_lambda,
              w_branch, w_out, norm_mem, norm_memkv, w_mem_q, w_mem_kv, w_mem_o, norm_final):
    W = dict(norm_mix=norm_mix, w_in=w_in, rwkv_mu=rwkv_mu, rwkv_w0=rwkv_w0, rwkv_w2=rwkv_w2,
             rwkv_a0=rwkv_a0, rwkv_a2=rwkv_a2, rwkv_kk=rwkv_kk, rwkv_ka=rwkv_ka, rwkv_rk=rwkv_rk,
             rwkv_ln_g=rwkv_ln_g, rwkv_ln_b=rwkv_ln_b, diff_lq1=diff_lq1, diff_lk1=diff_lk1,
             diff_lq2=diff_lq2, diff_lk2=diff_lk2, diff_subln=diff_subln, lru_conv_w=lru_conv_w,
             lru_conv_b=lru_conv_b, lru_wa=lru_wa, lru_ba=lru_ba, lru_wx=lru_wx, lru_bx=lru_bx,
             lru_lambda=lru_lambda, w_branch=w_branch, w_out=w_out, norm_mem=norm_mem,
             norm_memkv=norm_memkv, w_mem_q=w_mem_q, w_mem_kv=w_mem_kv, w_mem_o=w_mem_o)
    B = x_prompt.shape[0]
    f32 = jnp.float32

    xp = x_prompt
    p_k, p_v, p_mk, p_mv, p_S, p_sh, p_h, p_cv = [], [], [], [], [], [], [], []
    for l in range(DEPTH):
        mk, mv = _mem_kv(W, l, mem_prompt)
        xp, kl, vl, S_l, sh_l, h_l, cv_l = _layer(
            W, l, xp,
            jnp.zeros((B, W_SHIFT), xp.dtype), jnp.zeros((B, H_A, HS_A, HS_A), f32),
            jnp.zeros((B, CONV_W - 1, D_C), xp.dtype), jnp.zeros((B, D_C), f32),
            mk, mv, _diff_attn_prompt)
        p_k.append(kl); p_v.append(vl); p_mk.append(mk); p_mv.append(mv)
        p_S.append(S_l); p_sh.append(sh_l); p_h.append(h_l); p_cv.append(cv_l)
    y_prompt = _rms(xp, norm_final)

    xs = x_sample
    s_k, s_v, s_S, s_sh, s_h, s_cv = [], [], [], [], [], []
    for l in range(DEPTH):
        attn_fn = (lambda q, k, v, lam, layer=l:
                   _diff_attn_sample(q, k, v, lam, cache_diff_k, cache_diff_v, page_table, layer))
        xs, kl, vl, S_l, sh_l, h_l, cv_l = _layer(
            W, l, xs, state_rwkv_shift[l], state_rwkv_S[l], state_lru_conv[l], state_lru_h[l],
            cache_mem_k[l], cache_mem_v[l], attn_fn)
        s_k.append(kl); s_v.append(vl); s_S.append(S_l); s_sh.append(sh_l)
        s_h.append(h_l); s_cv.append(cv_l)
    y_sample = _rms(xs, norm_final)

    return (y_prompt, y_sample,
            jnp.stack(p_k), jnp.stack(p_v), jnp.stack(p_mk), jnp.stack(p_mv),
            jnp.stack(p_S), jnp.stack(p_sh), jnp.stack(p_h), jnp.stack(p_cv),
            jnp.stack(s_k), jnp.stack(s_v), jnp.stack(s_S), jnp.stack(s_sh),
            jnp.stack(s_h), jnp.stack(s_cv))
```

```python
import functools
import math

import jax
import jax.numpy as jnp
from jax import lax
from jax.experimental import pallas as pl
from jax.experimental.pallas import tpu as pltpu

F32 = jnp.float32
BF16 = jnp.bfloat16

D_MODEL = 1024
HS_A = 64
H_A = D_MODEL // HS_A
D_A = H_A * HS_A
R_W = 64
R_A = 64
LN_X_EPS = 64e-5
HD_B = 64
H_B = D_MODEL // (2 * HD_B)
VD_B = 2 * HD_B
SUBLN_EPS = 1e-5
D_C = D_MODEL
NB_C = 4
BS_C = D_C // NB_C
CONV_W = 4
LRU_C = 8.0
N_MEM = 256
H_MEM = 4
HD_MEM = D_MODEL // H_MEM
PAGE_SIZE = 128
W_SHIFT = 3 * D_A + R_W + R_A
RMS_EPS = 1e-6

LANES = 128
SUBLANES = 8
VMEM_LIMIT_BYTES = 56 * 1024 * 1024

PA_PAD = 4096
BLK_GA, BLK_Q, BLK_GB, BLK_XC, BLK_GC, BLK_GTS = 4, 5, 6, 7, 8, 9
N_PROJ = 12 * 1024

RWKV_CHUNK = 32
RWKV_TB = 128
PAGES_PER_STEP = 8


def _cparams(sem):
    return pltpu.CompilerParams(dimension_semantics=sem, vmem_limit_bytes=VMEM_LIMIT_BYTES)


def _dot1(a, b):
    return jnp.dot(a.astype(BF16), b.astype(BF16), preferred_element_type=F32)


def _split(a):
    hi = a.astype(BF16)
    lo = (a - hi.astype(F32)).astype(BF16)
    return hi, lo


def _dot3(a, b):
    ah, al = _split(a)
    bh, bl = _split(b)
    d = functools.partial(jnp.dot, preferred_element_type=F32)
    return d(ah, bh) + (d(ah, bl) + d(al, bh))


def _dot_exact_rhs(a, b_bf16):
    ah, al = _split(a)
    d = functools.partial(jnp.dot, preferred_element_type=F32)
    return d(ah, b_bf16) + d(al, b_bf16)


def _sigmoid(x):
    return 1.0 / (1.0 + jnp.exp(-x))


def _silu(x):
    return x * _sigmoid(x)


def _softplus(x):
    return jnp.maximum(x, 0.0) + jnp.log1p(jnp.exp(-jnp.abs(x)))


def _rms_rows(x, g, eps):
    ms = jnp.mean(x * x, axis=-1, keepdims=True)
    return x * lax.rsqrt(ms + eps) * g


def _segsum64(x, bd, precise):
    outs = []
    for q in range(x.shape[-1] // 256):
        xs = x[:, q * 256:(q + 1) * 256]
        if precise:
            outs.append(_dot_exact_rhs(xs, bd))
        else:
            outs.append(jnp.dot(xs.astype(BF16), bd, preferred_element_type=F32))
    return jnp.concatenate(outs, axis=-1)


def _rmsmm_kernel(x_ref, g_ref, w_ref, o_ref, xn_ref, *, eps, precise):
    @pl.when(pl.program_id(1) == 0)
    def _():
        xn_ref[...] = _rms_rows(x_ref[...], g_ref[...], eps).astype(xn_ref.dtype)

    if precise:
        o_ref[...] = _dot3(xn_ref[...], w_ref[...])
    else:
        o_ref[...] = jnp.dot(xn_ref[...], w_ref[...], preferred_element_type=F32)


def _rmsmm(x2, g, w, *, tm, tn, nsplit=1, precise=False, eps=RMS_EPS):
    m, d = x2.shape
    n = w.shape[1]
    assert m % tm == 0 and n % tn == 0
    if nsplit == 1:
        out_shape = jax.ShapeDtypeStruct((m, n), F32)
        out_spec = pl.BlockSpec((tm, tn), lambda i, j: (i, j))
    else:
        assert tn * nsplit == n
        out_shape = jax.ShapeDtypeStruct((nsplit, m, tn), F32)
        out_spec = pl.BlockSpec((None, tm, tn), lambda i, j: (j, i, 0))
    return pl.pallas_call(
        functools.partial(_rmsmm_kernel, eps=eps, precise=precise),
        grid=(m // tm, n // tn),
        in_specs=[pl.BlockSpec((tm, d), lambda i, j: (i, 0)),
                  pl.BlockSpec((1, d), lambda i, j: (0, 0)),
                  pl.BlockSpec((d, tn), lambda i, j: (0, j))],
        out_specs=out_spec,
        out_shape=out_shape,
        scratch_shapes=[pltpu.VMEM((tm, d), F32 if precise else BF16)],
        compiler_params=_cparams(("arbitrary", "arbitrary")),
        name="rms_matmul_hp" if precise else "rms_matmul",
    )(x2, g.reshape(1, d), w)


def _rwkv_rows(pm, w0, w2p, a0, a2p, kkp, ka, bd, dotf, precise):
    r = pm[:, 0:D_A]
    k = pm[:, D_A:2 * D_A]
    v = pm[:, 2 * D_A:3 * D_A]
    x = pm[:, 3 * D_A:3 * D_A + R_W + R_A]
    lane = lax.broadcasted_iota(jnp.int32, (1, R_W + R_A), 1)
    xw = jnp.where(lane < R_W, jnp.tanh(x), 0.0)
    wpre = w0 + dotf(xw, w2p)
    apre = a0 + dotf(x, a2p)
    lw = -jnp.exp(-_softplus(-wpre) - 0.5)
    a = _sigmoid(apre)
    kk = k * kkp
    nrm = jnp.sqrt(_segsum64(kk * kk, bd, precise))
    kk = kk / jnp.maximum(nrm, 1e-12)
    kmod = k * (1.0 + (a - 1.0) * ka)
    return r, kmod, v, -kk, kk * a, lw


def _rwkv_out(y, bon, v, ga, lng, lnb, bd, precise):
    mean = _segsum64(y, bd, True) * (1.0 / HS_A)
    d = y - mean
    var = _segsum64(d * d, bd, precise) * (1.0 / HS_A)
    yn = d * lax.rsqrt(var + LN_X_EPS) * lng + lnb
    bonus = _segsum64(bon, bd, precise) * v
    return (yn + bonus) * _silu(ga)


def _rwkv_prompt_kernel(pa_ref, ga_ref, mu_ref, w0_ref, w2p_ref, a0_ref, a2p_ref, kkp_ref, ka_ref, rk_ref,
                        lng_ref, lnb_ref, bd_ref, trin_ref, tot_ref,
                        oa_ref, s_ref,
                        prev_s, st_s, at_s, rt_s, bt_s, kt_s, bp_s, kp_s, v_s, wl_s, y_s, bon_s, *, tb, cl):
    t = pl.program_id(1)

    @pl.when(t == 0)
    def _():
        prev_s[...] = jnp.zeros_like(prev_s)
        st_s[...] = jnp.zeros_like(st_s)

    bd = bd_ref[...]
    pa = pa_ref[...]
    rowi = lax.broadcasted_iota(jnp.int32, (tb, 1), 0)
    prev = jnp.where(rowi == 0, prev_s[...], pltpu.roll(pa, 1, 0))
    pm = pa + mu_ref[...] * (prev - pa)
    prev_s[...] = pa[tb - 1:tb, :]

    r, kmod, v, aa, bb, lw = _rwkv_rows(pm, w0_ref[...], w2p_ref[...], a0_ref[...], a2p_ref[...],
                                        kkp_ref[...], ka_ref[...], bd, _dot1, False)
    lw_hi, lw_lo = _split(lw)
    mm = functools.partial(jnp.dot, preferred_element_type=F32)
    cum = mm(trin_ref[...], lw_hi) + mm(trin_ref[...], lw_lo)
    tot = mm(tot_ref[...], lw_hi) + mm(tot_ref[...], lw_lo)
    e = jnp.exp(cum)
    einv = jnp.exp(-cum)
    ed = jnp.exp(tot - cum)
    at_s[...] = (aa * jnp.exp(cum - lw)).astype(BF16)
    rt_s[...] = (r * e).astype(BF16)
    bt_s[...] = (bb * einv).astype(BF16)
    kt_s[...] = (kmod * einv).astype(BF16)
    bp_s[...] = (bb * ed).astype(BF16)
    kp_s[...] = (kmod * ed).astype(BF16)
    wl_s[...] = jnp.exp(tot)
    v_s[...] = v
    bon_s[...] = r * kmod * rk_ref[...]

    lane = lax.broadcasted_iota(jnp.int32, (1, LANES), 1)
    m0 = lane < HS_A
    gr = lax.broadcasted_iota(jnp.int32, (4 * cl, 2 * cl), 0)
    gc = lax.broadcasted_iota(jnp.int32, (4 * cl, 2 * cl), 1)
    grb = gr % cl
    gcb = gc % cl
    mask_g = gcb <= jnp.where(gr < 2 * cl, grb - 1, grb)
    is_k = lax.broadcasted_iota(jnp.int32, (cl, 2 * cl), 1) >= cl
    sr = lax.broadcasted_iota(jnp.int32, (LANES, LANES), 0)
    sc = lax.broadcasted_iota(jnp.int32, (LANES, LANES), 1)
    mask_s = (sr // HS_A) == (sc // HS_A)
    zl = jnp.zeros((cl, LANES), F32)

    def chunk(c, carry):
        rows = pl.ds(pl.multiple_of(c * cl, cl), cl)
        for p in range(D_A // LANES):
            lp = slice(p * LANES, (p + 1) * LANES)
            at = at_s[rows, lp]
            rt = rt_s[rows, lp]
            vv = v_s[rows, lp]
            zb = jnp.zeros_like(at)
            lhs4 = jnp.concatenate([jnp.where(m0, at, zb), jnp.where(m0, zb, at),
                                    jnp.where(m0, rt, zb), jnp.where(m0, zb, rt)], axis=0)
            rhs = jnp.concatenate([bt_s[rows, lp], kt_s[rows, lp]], axis=0)
            g = lax.dot_general(lhs4, rhs, (((1,), (1,)), ((), ())), preferred_element_type=F32)
            g = jnp.where(mask_g, g, 0.0)
            st = st_s[p]
            ps = lax.dot_general(jnp.concatenate([at, rt], axis=0), st.astype(BF16),
                                 (((1,), (1,)), ((), ())), preferred_element_type=F32)
            vm0 = jnp.where(m0, vv, 0.0)
            vm1 = vv - vm0
            gp0 = jnp.where(is_k, g[0:cl], 0.0)
            gp1 = jnp.where(is_k, g[cl:2 * cl], 0.0)
            u = (ps[0:cl] + _dot1(gp0, jnp.concatenate([zl, vm0], axis=0))
                 + _dot1(gp1, jnp.concatenate([zl, vm1], axis=0)))
            gab0 = g[0:cl]
            gab1 = g[cl:2 * cl]
            for s in range(cl - 1):
                coef = jnp.where(m0, gab0[:, s:s + 1], gab1[:, s:s + 1])
                u = u + coef * u[s:s + 1, :]
            um0 = jnp.where(m0, u, 0.0)
            um1 = u - um0
            y = (ps[cl:2 * cl] + _dot1(g[2 * cl:3 * cl], jnp.concatenate([um0, vm0], axis=0))
                 + _dot1(g[3 * cl:4 * cl], jnp.concatenate([um1, vm1], axis=0)))
            y_s[rows, lp] = y
            uv = jnp.concatenate([u, vv], axis=0).astype(BF16)
            bk = jnp.concatenate([bp_s[rows, lp], kp_s[rows, lp]], axis=0)
            upd = lax.dot_general(uv, bk, (((0,), (0,)), ((), ())), preferred_element_type=F32)
            wl = wl_s[pl.ds(pl.multiple_of(c * cl, cl), 1), lp]
            st_s[p] = jnp.where(mask_s, st * wl + upd, 0.0)
        return carry

    lax.fori_loop(0, tb // cl, chunk, 0)

    oa_ref[...] = _rwkv_out(y_s[...], bon_s[...], v_s[...], ga_ref[...], lng_ref[...], lnb_ref[...], bd, False)
    s_ref[...] = st_s[...]


def _block_mats(tb, cl):
    ri = jnp.arange(tb)[:, None]
    ci = jnp.arange(tb)[None, :]
    same = (ri // cl) == (ci // cl)
    trin = (same & (ci <= ri)).astype(BF16)
    tot = same.astype(BF16)
    return trin, tot


def _bd256():
    i = jnp.arange(256)
    return ((i[:, None] // HS_A) == (i[None, :] // HS_A)).astype(BF16)


def _rwkv_prompt(p3, l, W):
    b, t, _ = p3.shape
    tb, cl = RWKV_TB, RWKV_CHUNK
    assert t % tb == 0
    trin, tot = _block_mats(tb, cl)
    trin_l, tot_l = trin, tot
    row = lambda a: a.reshape(1, -1)
    const = lambda shape: pl.BlockSpec(shape, lambda i, j: (0,) * len(shape))
    in_specs = [pl.BlockSpec((None, tb, W_SHIFT), lambda i, j: (i, j, 0)),
                pl.BlockSpec((None, tb, D_A), lambda i, j: (i, j, BLK_GA)),
                const((1, W_SHIFT)), const((1, D_A)), const((R_W + R_A, D_A)), const((1, D_A)),
                const((R_W + R_A, D_A)), const((1, D_A)), const((1, D_A)), const((1, D_A)),
                const((1, D_A)), const((1, D_A)), const((256, 256)), const((tb, tb)), const((tb, tb))]
    out_specs = [pl.BlockSpec((None, tb, D_A), lambda i, j: (i, j, 0)),
                 pl.BlockSpec((None, D_A // LANES, LANES, LANES), lambda i, j: (i, 0, 0, 0))]
    out_shape = [jax.ShapeDtypeStruct((b, t, D_A), F32),
                 jax.ShapeDtypeStruct((b, D_A // LANES, LANES, LANES), F32)]
    scratch = [pltpu.VMEM((1, W_SHIFT), F32), pltpu.VMEM((D_A // LANES, LANES, LANES), F32)]
    scratch += [pltpu.VMEM((tb, D_A), BF16)] * 6
    scratch += [pltpu.VMEM((tb, D_A), F32)] * 4
    oa, sbd = pl.pallas_call(
        functools.partial(_rwkv_prompt_kernel, tb=tb, cl=cl),
        grid=(b, t // tb), in_specs=in_specs, out_specs=out_specs, out_shape=out_shape,
        scratch_shapes=scratch, compiler_params=_cparams(("arbitrary", "arbitrary")), name="rwkv_prompt",
    )(p3, p3, row(W['rwkv_mu'][l]), row(W['rwkv_w0'][l]), W['w2p_bf16'][l], row(W['rwkv_a0'][l]),
      W['a2p_bf16'][l], row(W['rwkv_kk'][l]), row(W['rwkv_ka'][l]), row(W['rwkv_rk'][l]),
      row(W['rwkv_ln_g'][l]), row(W['rwkv_ln_b'][l]), _bd256(), trin_l, tot_l)
    x = sbd.reshape(b, D_A // LANES, 2, HS_A, 2, HS_A)
    s_t = jnp.stack([x[:, :, 0, :, 0, :], x[:, :, 1, :, 1, :]], axis=2).reshape(b, H_A, HS_A, HS_A)
    return oa, s_t


def _rwkv_step_rows_kernel(pa_ref, prev_ref, mu_ref, w0_ref, w2p_ref, a0_ref, a2p_ref, kkp_ref, ka_ref, rk_ref,
                           bd_ref, r_ref, w_ref, k_ref, v_ref, aa_ref, bb_ref, bon_ref):
    pa = pa_ref[...]
    pm = pa + mu_ref[...] * (prev_ref[...] - pa)
    r, kmod, v, aa, bb, lw = _rwkv_rows(pm, w0_ref[...], w2p_ref[...], a0_ref[...], a2p_ref[...],
                                        kkp_ref[...], ka_ref[...], bd_ref[...], _dot3, True)
    r_ref[...] = r
    w_ref[...] = jnp.exp(lw)
    k_ref[...] = kmod
    v_ref[...] = v
    aa_ref[...] = aa
    bb_ref[...] = bb
    bon_ref[...] = r * kmod * rk_ref[...]


def _rwkv_step_state_kernel(s_ref, r_ref, w_ref, k_ref, aa_ref, bb_ref, vc_ref, so_ref, y_ref):
    s = s_ref[...]
    sa = jnp.sum(s * aa_ref[...], axis=-1, keepdims=True)
    sn = s * w_ref[...] + sa * bb_ref[...] + vc_ref[...] * k_ref[...]
    so_ref[...] = sn
    y_ref[...] = jnp.sum(sn * r_ref[...], axis=-1, keepdims=True)


def _rwkv_step_out_kernel(y_ref, bon_ref, v_ref, ga_ref, lng_ref, lnb_ref, bd_ref, o_ref):
    o_ref[...] = _rwkv_out(y_ref[...], bon_ref[...], v_ref[...], ga_ref[...], lng_ref[...], lnb_ref[...],
                           bd_ref[...], True)


def _rwkv_step(ps2, l, W, shift_prev, s0):
    b = ps2.shape[0]
    row = lambda a: a.reshape(1, -1)
    full = lambda shape: pl.BlockSpec(shape, lambda i: (0,) * len(shape))
    vec = jax.ShapeDtypeStruct((b, D_A), F32)
    outs = pl.pallas_call(
        _rwkv_step_rows_kernel, grid=(1,),
        in_specs=[pl.BlockSpec((b, W_SHIFT), lambda i: (0, 0)), full((b, W_SHIFT)), full((1, W_SHIFT)),
                  full((1, D_A)), full((R_W + R_A, D_A)), full((1, D_A)), full((R_W + R_A, D_A)),
                  full((1, D_A)), full((1, D_A)), full((1, D_A)), full((256, 256))],
        out_specs=[full((b, D_A))] * 7, out_shape=[vec] * 7,
        compiler_params=_cparams(("arbitrary",)), name="rwkv_step_rows",
    )(ps2, shift_prev, row(W['rwkv_mu'][l]), row(W['rwkv_w0'][l]), W['w2p'][l], row(W['rwkv_a0'][l]),
      W['a2p'][l], row(W['rwkv_kk'][l]), row(W['rwkv_ka'][l]), row(W['rwkv_rk'][l]), _bd256())
    r, w, k, v, aa, bb, bon = outs
    hrow = lambda a: a.reshape(b, H_A, 1, HS_A)
    rspec = pl.BlockSpec((None, H_A, 1, HS_A), lambda i: (i, 0, 0, 0))
    cspec = pl.BlockSpec((None, H_A, HS_A, 1), lambda i: (i, 0, 0, 0))
    sspec = pl.BlockSpec((None, H_A, HS_A, HS_A), lambda i: (i, 0, 0, 0))
    s_new, y = pl.pallas_call(
        _rwkv_step_state_kernel, grid=(b,),
        in_specs=[sspec, rspec, rspec, rspec, rspec, rspec, cspec],
        out_specs=[sspec, cspec],
        out_shape=[jax.ShapeDtypeStruct((b, H_A, HS_A, HS_A), F32), jax.ShapeDtypeStruct((b, H_A, HS_A, 1), F32)],
        compiler_params=_cparams(("arbitrary",)), name="rwkv_step_state",
    )(s0, hrow(r), hrow(w), hrow(k), hrow(aa), hrow(bb), v.reshape(b, H_A, HS_A, 1))
    ga = ps2[:, BLK_GA * 1024:(BLK_GA + 1) * 1024]
    oa = pl.pallas_call(
        _rwkv_step_out_kernel, grid=(1,),
        in_specs=[full((b, D_A))] * 4 + [full((1, D_A)), full((1, D_A)), full((256, 256))],
        out_specs=full((b, D_A)), out_shape=vec,
        compiler_params=_cparams(("arbitrary",)), name="rwkv_step_out",
    )(y.reshape(b, D_A), bon, v, ga, row(W['rwkv_ln_g'][l]), row(W['rwkv_ln_b'][l]), _bd256())
    return oa, s_new


def _lambda(lq_ref, lam_init):
    lq = lq_ref[...]
    e1 = jnp.sum(lq[0:1] * lq[1:2], axis=-1, keepdims=True)
    e2 = jnp.sum(lq[2:3] * lq[3:4], axis=-1, keepdims=True)
    return jnp.exp(e1) - jnp.exp(e2) + lam_init


def _diffattn_kernel(q_ref, k_ref, v_ref, gb_ref, lq_ref, sub_ref, o_ref, m_s, l_s, acc_s, *, tq, lam_init):
    i = pl.program_id(2)
    lane = lax.broadcasted_iota(jnp.int32, (1, LANES), 1)
    m0 = lane < HD_B
    q = q_ref[...] * (HD_B ** -0.5)
    qs = jnp.concatenate([jnp.where(m0, q, 0.0), jnp.where(m0, 0.0, q)], axis=0).astype(BF16)
    m_s[...] = jnp.full_like(m_s, -jnp.inf)
    l_s[...] = jnp.zeros_like(l_s)
    acc_s[...] = jnp.zeros_like(acc_s)

    def block(j, masked):
        rows = pl.ds(pl.multiple_of(j * tq, tq), tq)
        kb = k_ref[rows, :].astype(BF16)
        vb = v_ref[rows, :].astype(BF16)
        s = lax.dot_general(qs, kb, (((1,), (1,)), ((), ())), preferred_element_type=F32)
        if masked:
            rr = lax.broadcasted_iota(jnp.int32, (2 * tq, tq), 0)
            cc = lax.broadcasted_iota(jnp.int32, (2 * tq, tq), 1)
            rr = jnp.where(rr >= tq, rr - tq, rr)
            s = jnp.where(cc <= rr, s, -jnp.inf)
        m_old = m_s[...]
        m_new = jnp.maximum(m_old, jnp.max(s, axis=-1, keepdims=True))
        alpha = jnp.exp(m_old - m_new)
        p = jnp.exp(s - m_new)
        l_s[...] = alpha * l_s[...] + jnp.sum(p, axis=-1, keepdims=True)
        m_s[...] = m_new
        pb = p.astype(BF16)
        acc_s[0] = alpha[0:tq] * acc_s[0] + jnp.dot(pb[0:tq], vb, preferred_element_type=F32)
        acc_s[1] = alpha[tq:2 * tq] * acc_s[1] + jnp.dot(pb[tq:2 * tq], vb, preferred_element_type=F32)

    def body(j, carry):
        block(j, False)
        return carry

    lax.fori_loop(0, i, body, 0)
    block(i, True)

    lam = _lambda(lq_ref, lam_init)
    linv = 1.0 / l_s[...]
    o = acc_s[0] * linv[0:tq] - lam * (acc_s[1] * linv[tq:2 * tq])
    o = _rms_rows(o, sub_ref[...], SUBLN_EPS) * (1.0 - lam_init)
    o_ref[...] = o * _silu(gb_ref[...])


def _diffattn_prompt(p3, kv4, l, W, lam_init):
    b, t, _ = p3.shape
    tq = 256
    assert t % tq == 0
    lq = jnp.stack([W['diff_lq1'][l], W['diff_lk1'][l], W['diff_lq2'][l], W['diff_lk2'][l]])
    qb0 = BLK_Q * 1024 // LANES
    gb0 = BLK_GB * 1024 // LANES
    return pl.pallas_call(
        functools.partial(_diffattn_kernel, tq=tq, lam_init=lam_init),
        grid=(b, H_B, t // tq),
        in_specs=[pl.BlockSpec((None, tq, VD_B), lambda bi, h, i: (bi, i, qb0 + h)),
                  pl.BlockSpec((None, None, t, VD_B), lambda bi, h, i: (0, bi, 0, h)),
                  pl.BlockSpec((None, None, t, VD_B), lambda bi, h, i: (1, bi, 0, h)),
                  pl.BlockSpec((None, tq, VD_B), lambda bi, h, i: (bi, i, gb0 + h)),
                  pl.BlockSpec((4, HD_B), lambda bi, h, i: (0, 0)),
                  pl.BlockSpec((1, VD_B), lambda bi, h, i: (0, 0))],
        out_specs=pl.BlockSpec((None, tq, VD_B), lambda bi, h, i: (bi, i, h)),
        out_shape=jax.ShapeDtypeStruct((b, t, H_B * VD_B), F32),
        scratch_shapes=[pltpu.VMEM((2 * tq, 1), F32), pltpu.VMEM((2 * tq, 1), F32),
                        pltpu.VMEM((2, tq, VD_B), F32)],
        compiler_params=_cparams(("arbitrary", "arbitrary", "arbitrary")), name="diffattn_prompt",
    )(p3, kv4, kv4, p3, lq, W['diff_subln'][l].reshape(1, VD_B))


def _paged_kernel(pt_ref, q_ref, kn_ref, vn_ref, gb_ref, lq_ref, sub_ref, *rest, pps, lam_init):
    kpages = rest[0:pps]
    vpages = rest[pps:2 * pps]
    o_ref = rest[2 * pps]
    m_s, l_s, acc_s = rest[2 * pps + 1:]
    step = pl.program_id(1)
    nrow = 2 * H_B
    rr = lax.broadcasted_iota(jnp.int32, (nrow, D_MODEL), 0)
    ll = lax.broadcasted_iota(jnp.int32, (nrow, D_MODEL), 1)
    mask_h = (ll // VD_B) == (rr % H_B)
    mask_q = mask_h & (((ll // HD_B) % 2) == (rr // H_B))
    qbd = jnp.where(mask_q, q_ref[...] * (HD_B ** -0.5), 0.0)

    @pl.when(step == 0)
    def _():
        m_s[...] = jnp.full_like(m_s, -jnp.inf)
        l_s[...] = jnp.zeros_like(l_s)
        acc_s[...] = jnp.zeros_like(acc_s)

    qb = qbd.astype(BF16)
    s = jnp.concatenate(
        [lax.dot_general(qb, kp[...].astype(BF16), (((1,), (1,)), ((), ())), preferred_element_type=F32)
         for kp in kpages], axis=-1)
    m_old = m_s[...]
    m_new = jnp.maximum(m_old, jnp.max(s, axis=-1, keepdims=True))
    alpha = jnp.exp(m_old - m_new)
    p = jnp.exp(s - m_new)
    l_s[...] = alpha * l_s[...] + jnp.sum(p, axis=-1, keepdims=True)
    m_s[...] = m_new
    pb = p.astype(BF16)
    acc = alpha * acc_s[...]
    for u, vp in enumerate(vpages):
        acc = acc + jnp.dot(pb[:, u * PAGE_SIZE:(u + 1) * PAGE_SIZE], vp[...].astype(BF16),
                            preferred_element_type=F32)
    acc_s[...] = acc

    @pl.when(step == pl.num_programs(1) - 1)
    def _():
        s_new = jnp.sum(qbd * kn_ref[...], axis=-1, keepdims=True)
        m_o = m_s[...]
        m_f = jnp.maximum(m_o, s_new)
        a_f = jnp.exp(m_o - m_f)
        p_new = jnp.exp(s_new - m_f)
        l_f = a_f * l_s[...] + p_new
        acc_f = a_f * acc_s[...] + p_new * vn_ref[...]
        o16 = jnp.where(mask_h, acc_f / l_f, 0.0)
        o0 = jnp.sum(o16[0:H_B], axis=0, keepdims=True)
        o1 = jnp.sum(o16[H_B:2 * H_B], axis=0, keepdims=True)
        o = o0 - _lambda(lq_ref, lam_init) * o1
        sub = sub_ref[...]
        tiles = []
        for h in range(H_B):
            tiles.append(_rms_rows(o[:, h * VD_B:(h + 1) * VD_B], sub, SUBLN_EPS))
        o = jnp.concatenate(tiles, axis=-1) * (1.0 - lam_init)
        o_ref[...] = o * _silu(gb_ref[...])


def _paged_attn(q3, kn3, vn3, gb3, cache_k4, cache_v4, page_table, l, W, lam_init):
    b = q3.shape[0]
    n_pages = page_table.shape[1]
    pps = PAGES_PER_STEP
    while n_pages % pps:
        pps //= 2
    lq = jnp.stack([W['diff_lq1'][l], W['diff_lk1'][l], W['diff_lq2'][l], W['diff_lk2'][l]])
    tok = pl.BlockSpec((None, 1, D_MODEL), lambda bi, s, pt: (bi, 0, 0))

    def page_spec(u):
        return pl.BlockSpec((None, None, PAGE_SIZE, D_MODEL), lambda bi, s, pt: (l, pt[bi, s * pps + u], 0, 0))

    grid_spec = pltpu.PrefetchScalarGridSpec(
        num_scalar_prefetch=1, grid=(b, n_pages // pps),
        in_specs=[tok, tok, tok, tok,
                  pl.BlockSpec((4, HD_B), lambda bi, s, pt: (0, 0)),
                  pl.BlockSpec((1, VD_B), lambda bi, s, pt: (0, 0))]
                 + [page_spec(u) for u in range(pps)] + [page_spec(u) for u in range(pps)],
        out_specs=tok,
        scratch_shapes=[pltpu.VMEM((2 * H_B, 1), F32), pltpu.VMEM((2 * H_B, 1), F32),
                        pltpu.VMEM((2 * H_B, D_MODEL), F32)])
    return pl.pallas_call(
        functools.partial(_paged_kernel, pps=pps, lam_init=lam_init),
        grid_spec=grid_spec, out_shape=jax.ShapeDtypeStruct((b, 1, D_MODEL), F32),
        compiler_params=_cparams(("arbitrary", "arbitrary")), name="paged_diffattn",
    )(page_table, q3, kn3, vn3, gb3, lq, W['diff_subln'][l].reshape(1, VD_B),
      *([cache_k4] * pps), *([cache_v4] * pps))


def _lru_gates(xconv, wa_ref, ba, wx_ref, bx, lam, dotf):
    pr, pi = [], []
    for n in range(NB_C):
        xb = xconv[:, n * BS_C:(n + 1) * BS_C]
        pr.append(dotf(xb, wa_ref[n]))
        pi.append(dotf(xb, wx_ref[n]))
    gate_r = _sigmoid(jnp.concatenate(pr, axis=-1) + ba)
    gate_i = _sigmoid(jnp.concatenate(pi, axis=-1) + bx)
    log_a = -LRU_C * gate_r * _softplus(-lam)
    a = jnp.exp(log_a)
    bv = jnp.sqrt(1.0 - jnp.exp(2.0 * log_a)) * (gate_i * xconv)
    return a, bv


def _lru_prompt_kernel(xc_ref, gc_ref, cp_ref, h0_ref, cw_ref, cb_ref, wa_ref, ba_ref, wx_ref, bx_ref, lam_ref,
                       oc_ref, h_ref, ext_s, hc_s, *, tt):
    t = pl.program_id(1)

    @pl.when(t == 0)
    def _():
        ext_s[0:SUBLANES, :] = cp_ref[...]
        hc_s[...] = h0_ref[...]

    xc = xc_ref[...]
    ext_s[SUBLANES:SUBLANES + tt, :] = xc
    cw = cw_ref[...]
    xconv = cw[3:4] * xc + cb_ref[...]
    for j in range(1, CONV_W):
        xconv = xconv + cw[3 - j:4 - j] * ext_s[pl.ds(SUBLANES - j, tt), :]
    ext_s[0:SUBLANES, :] = xc[tt - SUBLANES:tt, :]

    a, bv = _lru_gates(xconv, wa_ref, ba_ref[...], wx_ref, bx_ref[...], lam_ref[...], _dot1)
    rowi = lax.broadcasted_iota(jnp.int32, (tt, 1), 0)
    s = 1
    while s < tt:
        keep = rowi >= s
        a_sh = jnp.where(keep, pltpu.roll(a, s, 0), 1.0)
        b_sh = jnp.where(keep, pltpu.roll(bv, s, 0), 0.0)
        bv = a * b_sh + bv
        a = a * a_sh
        s *= 2
    h = a * hc_s[...] + bv
    hc_s[...] = h[tt - 1:tt, :]
    h_ref[...] = h[tt - 1:tt, :]
    oc_ref[...] = h * _silu(gc_ref[...])


def _lru_prompt(p3, l, W, conv_prev8, h0):
    b, t, _ = p3.shape
    tt = 256
    assert t % tt == 0
    row = lambda a: a.reshape(1, -1)
    const = lambda shape: pl.BlockSpec(shape, lambda i, j: (0,) * len(shape))
    oc, h = pl.pallas_call(
        functools.partial(_lru_prompt_kernel, tt=tt),
        grid=(b, t // tt),
        in_specs=[pl.BlockSpec((None, tt, D_C), lambda i, j: (i, j, BLK_XC)),
                  pl.BlockSpec((None, tt, D_C), lambda i, j: (i, j, BLK_GC)),
                  pl.BlockSpec((None, SUBLANES, D_C), lambda i, j: (i, 0, 0)),
                  pl.BlockSpec((None, 1, D_C), lambda i, j: (i, 0, 0)),
                  const((CONV_W, D_C)), const((1, D_C)), const((NB_C, BS_C, BS_C)), const((1, D_C)),
                  const((NB_C, BS_C, BS_C)), const((1, D_C)), const((1, D_C))],
        out_specs=[pl.BlockSpec((None, tt, D_C), lambda i, j: (i, j, 0)),
                   pl.BlockSpec((None, 1, D_C), lambda i, j: (i, 0, 0))],
        out_shape=[jax.ShapeDtypeStruct((b, t, D_C), F32), jax.ShapeDtypeStruct((b, 1, D_C), F32)],
        scratch_shapes=[pltpu.VMEM((tt + SUBLANES, D_C), F32), pltpu.VMEM((1, D_C), F32)],
        compiler_params=_cparams(("arbitrary", "arbitrary")), name="lru_prompt",
    )(p3, p3, conv_prev8, h0, W['lru_conv_w'][l], row(W['lru_conv_b'][l]), W['wa_bf16'][l], row(W['lru_ba'][l]),
      W['wx_bf16'][l], row(W['lru_bx'][l]), row(W['lru_lambda'][l]))
    return oc, h.reshape(b, D_C)


def _lru_step_kernel(xc_ref, gc_ref, c0_ref, c1_ref, c2_ref, h0_ref, cw_ref, cb_ref, wa_ref, ba_ref, wx_ref,
                     bx_ref, lam_ref, oc_ref, h_ref):
    cw = cw_ref[...]
    xconv = (cw[0:1] * c0_ref[...] + cw[1:2] * c1_ref[...] + cw[2:3] * c2_ref[...] + cw[3:4] * xc_ref[...]
             + cb_ref[...])
    a, bv = _lru_gates(xconv, wa_ref, ba_ref[...], wx_ref, bx_ref[...], lam_ref[...], _dot3)
    h = a * h0_ref[...] + bv
    h_ref[...] = h
    oc_ref[...] = h * _silu(gc_ref[...])


def _lru_step(ps2, l, W, conv_prev, h0):
    b = ps2.shape[0]
    row = lambda a: a.reshape(1, -1)
    full = lambda shape: pl.BlockSpec(shape, lambda i: (0,) * len(shape))
    xc = ps2[:, BLK_XC * 1024:(BLK_XC + 1) * 1024]
    gc = ps2[:, BLK_GC * 1024:(BLK_GC + 1) * 1024]
    vec = jax.ShapeDtypeStruct((b, D_C), F32)
    oc, h = pl.pallas_call(
        _lru_step_kernel, grid=(1,),
        in_specs=[full((b, D_C))] * 6 + [full((CONV_W, D_C)), full((1, D_C)), full((NB_C, BS_C, BS_C)),
                                          full((1, D_C)), full((NB_C, BS_C, BS_C)), full((1, D_C)), full((1, D_C))],
        out_specs=[full((b, D_C))] * 2, out_shape=[vec, vec],
        compiler_params=_cparams(("arbitrary",)), name="lru_step",
    )(xc, gc, conv_prev[:, 0], conv_prev[:, 1], conv_prev[:, 2], h0, W['lru_conv_w'][l], row(W['lru_conv_b'][l]),
      W['lru_wa'][l], row(W['lru_ba'][l]), W['lru_wx'][l], row(W['lru_bx'][l]), row(W['lru_lambda'][l]))
    conv_new = jnp.stack([conv_prev[:, 1], conv_prev[:, 2], xc], axis=1)
    return oc, h, conv_new


def _merge_kernel(x_ref, oa_ref, ob_ref, oc_ref, g_ref, wb_ref, wo_ref, nm_ref, wq_ref, x1_ref, q_ref):
    acc = None
    for n, o_ref in enumerate((oa_ref, ob_ref, oc_ref)):
        tproj = jnp.dot(o_ref[...].astype(BF16), wb_ref[n], preferred_element_type=F32)
        term = _sigmoid(g_ref[:, n * D_MODEL:(n + 1) * D_MODEL]) * tproj
        acc = term if acc is None else acc + term
    x1 = x_ref[...] + jnp.dot(acc.astype(BF16), wo_ref[...], preferred_element_type=F32)
    x1_ref[...] = x1
    xn = _rms_rows(x1, nm_ref[...], RMS_EPS)
    q_ref[...] = jnp.dot(xn.astype(BF16), wq_ref[...], preferred_element_type=F32)


def _merge(x2, oa2, ob2, oc2, p2, l, W, tm):
    m = x2.shape[0]
    assert m % tm == 0
    tok = pl.BlockSpec((tm, D_MODEL), lambda i: (i, 0))
    const = lambda shape: pl.BlockSpec(shape, lambda i: (0,) * len(shape))
    out = jax.ShapeDtypeStruct((m, D_MODEL), F32)
    return pl.pallas_call(
        _merge_kernel, grid=(m // tm,),
        in_specs=[tok, tok, tok, tok, pl.BlockSpec((tm, 3 * D_MODEL), lambda i: (i, BLK_GTS // 3)),
                  const((3, D_MODEL, D_MODEL)), const((D_MODEL, D_MODEL)), const((1, D_MODEL)),
                  const((D_MODEL, D_MODEL))],
        out_specs=[tok, tok], out_shape=[out, out],
        compiler_params=_cparams(("arbitrary",)), name="merge_outproj",
    )(x2, oa2, ob2, oc2, p2, W['wb_bf16'][l], W['wout_bf16'][l], W['norm_mem'][l].reshape(1, D_MODEL),
      W['wq_bf16'][l])


def _memattn_kernel(q_ref, mk_ref, mv_ref, o_ref):
    outs = []
    for h in range(H_MEM):
        sl = slice(h * HD_MEM, (h + 1) * HD_MEM)
        s = lax.dot_general(q_ref[:, sl].astype(BF16), mk_ref[:, sl].astype(BF16), (((1,), (1,)), ((), ())),
                            preferred_element_type=F32) * (HD_MEM ** -0.5)
        m = jnp.max(s, axis=-1, keepdims=True)
        p = jnp.exp(s - m)
        p = p / jnp.sum(p, axis=-1, keepdims=True)
        outs.append(jnp.dot(p.astype(BF16), mv_ref[:, sl].astype(BF16), preferred_element_type=F32))
    o_ref[...] = jnp.concatenate(outs, axis=-1)


def _memattn(q3, mk3, mv3, tq):
    b, t, _ = q3.shape
    assert t % tq == 0
    tok = pl.BlockSpec((None, tq, D_MODEL), lambda i, j: (i, j, 0))
    mem = pl.BlockSpec((None, N_MEM, D_MODEL), lambda i, j: (i, 0, 0))
    return pl.pallas_call(
        _memattn_kernel, grid=(b, t // tq), in_specs=[tok, mem, mem], out_specs=tok,
        out_shape=jax.ShapeDtypeStruct((b, t, D_MODEL), F32),
        compiler_params=_cparams(("arbitrary", "arbitrary")), name="mem_attn",
    )(q3, mk3, mv3)


def _oproj_kernel(x_ref, o_ref, w_ref, g_ref, y_ref, *, final):
    x2 = x_ref[...] + jnp.dot(o_ref[...].astype(BF16), w_ref[...], preferred_element_type=F32)
    y_ref[...] = _rms_rows(x2, g_ref[...], RMS_EPS) if final else x2


def _oproj(x2, o2, w_bf16, g, final, tm):
    m = x2.shape[0]
    assert m % tm == 0
    tok = pl.BlockSpec((tm, D_MODEL), lambda i: (i, 0))
    return pl.pallas_call(
        functools.partial(_oproj_kernel, final=final), grid=(m // tm,),
        in_specs=[tok, tok, pl.BlockSpec((D_MODEL, D_MODEL), lambda i: (0, 0)),
                  pl.BlockSpec((1, D_MODEL), lambda i: (0, 0))],
        out_specs=tok, out_shape=jax.ShapeDtypeStruct((m, D_MODEL), F32),
        compiler_params=_cparams(("arbitrary",)), name="mem_outproj",
    )(x2, o2, w_bf16, g.reshape(1, D_MODEL))


def _prep_weights(W):
    w_in = W['w_in']
    depth = w_in.shape[0]
    offs = [0]
    for s in (W_SHIFT, D_A, 1024, 1024, D_MODEL, D_MODEL, D_C, D_C, 3 * D_MODEL):
        offs.append(offs[-1] + s)
    seg = lambda n: w_in[:, :, offs[n]:offs[n + 1]]
    pa, ga, qb, kb, vb, gb, xc, gc, gts = (seg(n) for n in range(9))
    pad = jnp.zeros((depth, D_MODEL, PA_PAD - W_SHIFT), w_in.dtype)
    wp = jnp.concatenate([pa, pad, ga, qb, gb, xc, gc, gts], axis=-1)
    wkv = jnp.concatenate([kb, vb], axis=-1)
    z = jnp.zeros((depth, R_W, D_A), F32)
    out = dict(W)
    out['wp'], out['wkv'] = wp, wkv
    out['wp_bf16'], out['wkv_bf16'] = wp.astype(BF16), wkv.astype(BF16)
    out['w2p'] = jnp.concatenate([W['rwkv_w2'], z], axis=1)
    out['a2p'] = jnp.concatenate([z, W['rwkv_a2']], axis=1)
    out['w2p_bf16'], out['a2p_bf16'] = out['w2p'].astype(BF16), out['a2p'].astype(BF16)
    out['wa_bf16'], out['wx_bf16'] = W['lru_wa'].astype(BF16), W['lru_wx'].astype(BF16)
    out['wb_bf16'] = W['w_branch'].astype(BF16)
    out['wout_bf16'] = W['w_out'].astype(BF16)
    out['wq_bf16'] = W['w_mem_q'].astype(BF16)
    out['wmkv_bf16'] = W['w_mem_kv'].astype(BF16)
    out['wmo_bf16'] = W['w_mem_o'].astype(BF16)
    return out


def _prompt_layer(W, l, x3, mem2, last):
    b, t, _ = x3.shape
    x2 = x3.reshape(b * t, D_MODEL)
    lam_init = 0.8 - 0.6 * math.exp(-0.3 * l)
    mkv = _rmsmm(mem2, W['norm_memkv'][l], W['wmkv_bf16'][l], tm=min(1024, mem2.shape[0]), tn=D_MODEL, nsplit=2)
    tm = min(1024, b * t)
    p2 = _rmsmm(x2, W['norm_mix'][l], W['wp_bf16'][l], tm=tm, tn=2048)
    kv = _rmsmm(x2, W['norm_mix'][l], W['wkv_bf16'][l], tm=tm, tn=D_MODEL, nsplit=2)
    p3 = p2.reshape(b, t, N_PROJ)
    kv4 = kv.reshape(2, b, t, D_MODEL)
    oa, s_t = _rwkv_prompt(p3, l, W)
    ob = _diffattn_prompt(p3, kv4, l, W, lam_init)
    oc, h_t = _lru_prompt(p3, l, W, jnp.zeros((b, SUBLANES, D_C), F32), jnp.zeros((b, 1, D_C), F32))
    tmm = min(256, b * t)
    x1, q = _merge(x2, oa.reshape(b * t, D_A), ob.reshape(b * t, D_MODEL), oc.reshape(b * t, D_C), p2, l, W, tmm)
    mk3 = mkv[0].reshape(b, N_MEM, D_MODEL)
    mv3 = mkv[1].reshape(b, N_MEM, D_MODEL)
    o = _memattn(q.reshape(b, t, D_MODEL), mk3, mv3, min(512, t))
    g = W['norm_final'] if last else W['norm_mem'][l]
    xo = _oproj(x1, o.reshape(b * t, D_MODEL), W['wmo_bf16'][l], g, last, min(512, b * t))
    k_l = kv4[0].reshape(b, t, H_B, 2 * HD_B)
    v_l = kv4[1].reshape(b, t, H_B, VD_B)
    mk_l = mk3.reshape(b, N_MEM, H_MEM, HD_MEM)
    mv_l = mv3.reshape(b, N_MEM, H_MEM, HD_MEM)
    shift = p3[:, t - 1, 0:W_SHIFT]
    conv = p3[:, t - (CONV_W - 1):, BLK_XC * 1024:(BLK_XC + 1) * 1024]
    return xo.reshape(b, t, D_MODEL), (k_l, v_l, mk_l, mv_l, s_t, shift, h_t, conv)


def _sample_layer(W, l, xs3, shift_prev, s0, conv_prev, h0, mem_k, mem_v, cache_k4, cache_v4, page_table, last):
    b = xs3.shape[0]
    x2 = xs3.reshape(b, D_MODEL)
    lam_init = 0.8 - 0.6 * math.exp(-0.3 * l)
    p2 = _rmsmm(x2, W['norm_mix'][l], W['wp'][l], tm=b, tn=1024, precise=True)
    kv = _rmsmm(x2, W['norm_mix'][l], W['wkv'][l], tm=b, tn=D_MODEL, nsplit=2, precise=True)
    oa, s_new = _rwkv_step(p2, l, W, shift_prev, s0)
    q3 = p2[:, BLK_Q * 1024:(BLK_Q + 1) * 1024].reshape(b, 1, D_MODEL)
    gb3 = p2[:, BLK_GB * 1024:(BLK_GB + 1) * 1024].reshape(b, 1, D_MODEL)
    ob = _paged_attn(q3, kv[0].reshape(b, 1, D_MODEL), kv[1].reshape(b, 1, D_MODEL), gb3, cache_k4, cache_v4,
                     page_table, l, W, lam_init)
    oc, h_new, conv_new = _lru_step(p2, l, W, conv_prev, h0)
    x1, q = _merge(x2, oa, ob.reshape(b, D_MODEL), oc, p2, l, W, b)
    o = _memattn(q.reshape(b, 1, D_MODEL), mem_k.reshape(b, N_MEM, D_MODEL), mem_v.reshape(b, N_MEM, D_MODEL), 1)
    g = W['norm_final'] if last else W['norm_mem'][l]
    xo = _oproj(x1, o.reshape(b, D_MODEL), W['wmo_bf16'][l], g, last, b)
    k_l = kv[0].reshape(b, 1, H_B, 2 * HD_B)
    v_l = kv[1].reshape(b, 1, H_B, VD_B)
    shift = p2[:, 0:W_SHIFT]
    return xo.reshape(b, 1, D_MODEL), (k_l, v_l, s_new, shift, h_new, conv_new)


def kernel(x_prompt, x_sample, cache_diff_k, cache_diff_v, cache_mem_k, cache_mem_v, state_rwkv_S, state_rwkv_shift, state_lru_h, state_lru_conv, page_table, mem_prompt, norm_mix, w_in, rwkv_mu, rwkv_w0, rwkv_w2, rwkv_a0, rwkv_a2, rwkv_kk, rwkv_ka, rwkv_rk, rwkv_ln_g, rwkv_ln_b, diff_lq1, diff_lk1, diff_lq2, diff_lk2, diff_subln, lru_conv_w, lru_conv_b, lru_wa, lru_ba, lru_wx, lru_bx, lru_lambda, w_branch, w_out, norm_mem, norm_memkv, w_mem_q, w_mem_kv, w_mem_o, norm_final):
    W = _prep_weights(dict(
        norm_mix=norm_mix, w_in=w_in, rwkv_mu=rwkv_mu, rwkv_w0=rwkv_w0, rwkv_w2=rwkv_w2, rwkv_a0=rwkv_a0,
        rwkv_a2=rwkv_a2, rwkv_kk=rwkv_kk, rwkv_ka=rwkv_ka, rwkv_rk=rwkv_rk, rwkv_ln_g=rwkv_ln_g,
        rwkv_ln_b=rwkv_ln_b, diff_lq1=diff_lq1, diff_lk1=diff_lk1, diff_lq2=diff_lq2, diff_lk2=diff_lk2,
        diff_subln=diff_subln, lru_conv_w=lru_conv_w, lru_conv_b=lru_conv_b, lru_wa=lru_wa, lru_ba=lru_ba,
        lru_wx=lru_wx, lru_bx=lru_bx, lru_lambda=lru_lambda, w_branch=w_branch, w_out=w_out, norm_mem=norm_mem,
        norm_memkv=norm_memkv, w_mem_q=w_mem_q, w_mem_kv=w_mem_kv, w_mem_o=w_mem_o, norm_final=norm_final))
    depth = w_in.shape[0]
    bp = x_prompt.shape[0]
    mem2 = mem_prompt.reshape(bp * N_MEM, D_MODEL)

    xp = x_prompt
    p_outs = []
    for l in range(depth):
        xp, outs = _prompt_layer(W, l, xp, mem2, l == depth - 1)
        p_outs.append(outs)

    n_pool = cache_diff_k.shape[1]
    ck4 = cache_diff_k.reshape(depth, n_pool, PAGE_SIZE, D_MODEL)
    cv4 = cache_diff_v.reshape(depth, n_pool, PAGE_SIZE, D_MODEL)
    xs = x_sample
    s_outs = []
    for l in range(depth):
        xs, outs = _sample_layer(W, l, xs, state_rwkv_shift[l], state_rwkv_S[l], state_lru_conv[l], state_lru_h[l],
                                 cache_mem_k[l], cache_mem_v[l], ck4, cv4, page_table, l == depth - 1)
        s_outs.append(outs)

    stack = lambda outs, n: jnp.stack([o[n] for o in outs])
    return (xp, xs,
            stack(p_outs, 0), stack(p_outs, 1), stack(p_outs, 2), stack(p_outs, 3),
            stack(p_outs, 4), stack(p_outs, 5), stack(p_outs, 6), stack(p_outs, 7),
            stack(s_outs, 0), stack(s_outs, 1), stack(s_outs, 2), stack(s_outs, 3),
            stack(s_outs, 4), stack(s_outs, 5))
```

```python
import functools
import math

import jax
import jax.numpy as jnp
from jax import lax
from jax.experimental import pallas as pl
from jax.experimental.pallas import tpu as pltpu

F32 = jnp.float32
BF16 = jnp.bfloat16

D_MODEL = 1024
HS_A = 64
H_A = D_MODEL // HS_A
D_A = H_A * HS_A
R_W = 64
R_A = 64
LN_X_EPS = 64e-5
HD_B = 64
H_B = D_MODEL // (2 * HD_B)
VD_B = 2 * HD_B
SUBLN_EPS = 1e-5
D_C = D_MODEL
NB_C = 4
BS_C = D_C // NB_C
CONV_W = 4
LRU_C = 8.0
N_MEM = 256
H_MEM = 4
HD_MEM = D_MODEL // H_MEM
PAGE_SIZE = 128
W_SHIFT = 3 * D_A + R_W + R_A
RMS_EPS = 1e-6

LANES = 128
SUBLANES = 8
VMEM_LIMIT_BYTES = 56 * 1024 * 1024

PA_PAD = 4096
BLK_GA, BLK_Q, BLK_GB, BLK_XC, BLK_GC, BLK_GTS = 4, 5, 6, 7, 8, 9
N_PROJ = 12 * 1024

RWKV_CHUNK = 32
RWKV_TB = 128
PAGES_PER_STEP = 8
DIFF_HEADS_PER_STEP = 4


def _cparams(sem):
    return pltpu.CompilerParams(dimension_semantics=sem, vmem_limit_bytes=VMEM_LIMIT_BYTES)


def _dot1(a, b):
    return jnp.dot(a.astype(BF16), b.astype(BF16), preferred_element_type=F32)


def _split(a):
    hi = a.astype(BF16)
    lo = (a - hi.astype(F32)).astype(BF16)
    return hi, lo


def _dot3(a, b):
    ah, al = _split(a)
    bh, bl = _split(b)
    d = functools.partial(jnp.dot, preferred_element_type=F32)
    return d(ah, bh) + (d(ah, bl) + d(al, bh))


def _dot_exact_rhs(a, b_bf16):
    ah, al = _split(a)
    d = functools.partial(jnp.dot, preferred_element_type=F32)
    return d(ah, b_bf16) + d(al, b_bf16)


def _sigmoid(x):
    return 1.0 / (1.0 + jnp.exp(-x))


def _silu(x):
    return x * _sigmoid(x)


def _softplus(x):
    return jnp.maximum(x, 0.0) + jnp.log1p(jnp.exp(-jnp.abs(x)))


def _rms_rows(x, g, eps):
    ms = jnp.mean(x * x, axis=-1, keepdims=True)
    return x * lax.rsqrt(ms + eps) * g


def _segsum64(x, bd, precise):
    outs = []
    for q in range(x.shape[-1] // 256):
        xs = x[:, q * 256:(q + 1) * 256]
        if precise:
            outs.append(_dot_exact_rhs(xs, bd))
        else:
            outs.append(jnp.dot(xs.astype(BF16), bd, preferred_element_type=F32))
    return jnp.concatenate(outs, axis=-1)


def _rmsmm_kernel(x_ref, g_ref, w_ref, o_ref, xn_ref, *, eps, precise):
    @pl.when(pl.program_id(1) == 0)
    def _():
        xn_ref[...] = _rms_rows(x_ref[...], g_ref[...], eps).astype(xn_ref.dtype)

    if precise:
        o_ref[...] = _dot3(xn_ref[...], w_ref[...])
    else:
        o_ref[...] = jnp.dot(xn_ref[...], w_ref[...], preferred_element_type=F32)


def _rmsmm(x2, g, w, *, tm, tn, nsplit=1, precise=False, eps=RMS_EPS):
    m, d = x2.shape
    n = w.shape[1]
    assert m % tm == 0 and n % tn == 0
    if nsplit == 1:
        out_shape = jax.ShapeDtypeStruct((m, n), F32)
        out_spec = pl.BlockSpec((tm, tn), lambda i, j: (i, j))
    else:
        assert tn * nsplit == n
        out_shape = jax.ShapeDtypeStruct((nsplit, m, tn), F32)
        out_spec = pl.BlockSpec((None, tm, tn), lambda i, j: (j, i, 0))
    return pl.pallas_call(
        functools.partial(_rmsmm_kernel, eps=eps, precise=precise),
        grid=(m // tm, n // tn),
        in_specs=[pl.BlockSpec((tm, d), lambda i, j: (i, 0)),
                  pl.BlockSpec((1, d), lambda i, j: (0, 0)),
                  pl.BlockSpec((d, tn), lambda i, j: (0, j))],
        out_specs=out_spec,
        out_shape=out_shape,
        scratch_shapes=[pltpu.VMEM((tm, d), F32 if precise else BF16)],
        compiler_params=_cparams(("arbitrary", "arbitrary")),
        name="rms_matmul_hp" if precise else "rms_matmul",
    )(x2, g.reshape(1, d), w)


def _rwkv_rows(pm, w0, w2p, a0, a2p, kkp, ka, bd, dotf, precise):
    r = pm[:, 0:D_A]
    k = pm[:, D_A:2 * D_A]
    v = pm[:, 2 * D_A:3 * D_A]
    x = pm[:, 3 * D_A:3 * D_A + R_W + R_A]
    lane = lax.broadcasted_iota(jnp.int32, (1, R_W + R_A), 1)
    xw = jnp.where(lane < R_W, jnp.tanh(x), 0.0)
    wpre = w0 + dotf(xw, w2p)
    apre = a0 + dotf(x, a2p)
    lw = -jnp.exp(-_softplus(-wpre) - 0.5)
    a = _sigmoid(apre)
    kk = k * kkp
    nrm = jnp.sqrt(_segsum64(kk * kk, bd, precise))
    kk = kk / jnp.maximum(nrm, 1e-12)
    kmod = k * (1.0 + (a - 1.0) * ka)
    return r, kmod, v, -kk, kk * a, lw


def _rwkv_out(y, bon, v, ga, lng, lnb, bd, precise):
    mean = _segsum64(y, bd, True) * (1.0 / HS_A)
    d = y - mean
    var = _segsum64(d * d, bd, precise) * (1.0 / HS_A)
    yn = d * lax.rsqrt(var + LN_X_EPS) * lng + lnb
    bonus = _segsum64(bon, bd, precise) * v
    return (yn + bonus) * _silu(ga)


def _rwkv_prompt_kernel(pa_ref, ga_ref, mu_ref, w0_ref, w2p_ref, a0_ref, a2p_ref, kkp_ref, ka_ref, rk_ref,
                        lng_ref, lnb_ref, bd_ref, trin_ref, tot_ref,
                        oa_ref, s_ref,
                        prev_s, st_s, at_s, rt_s, bt_s, kt_s, bp_s, kp_s, v_s, wl_s, y_s, bon_s, *, tb, cl):
    t = pl.program_id(1)

    @pl.when(t == 0)
    def _():
        prev_s[...] = jnp.zeros_like(prev_s)
        st_s[...] = jnp.zeros_like(st_s)

    bd = bd_ref[...]
    pa = pa_ref[...]
    rowi = lax.broadcasted_iota(jnp.int32, (tb, 1), 0)
    prev = jnp.where(rowi == 0, prev_s[...], pltpu.roll(pa, 1, 0))
    pm = pa + mu_ref[...] * (prev - pa)
    prev_s[...] = pa[tb - 1:tb, :]

    r, kmod, v, aa, bb, lw = _rwkv_rows(pm, w0_ref[...], w2p_ref[...], a0_ref[...], a2p_ref[...],
                                        kkp_ref[...], ka_ref[...], bd, _dot1, False)
    lw_hi, lw_lo = _split(lw)
    mm = functools.partial(jnp.dot, preferred_element_type=F32)
    cum = mm(trin_ref[...], lw_hi) + mm(trin_ref[...], lw_lo)
    tot = mm(tot_ref[...], lw_hi) + mm(tot_ref[...], lw_lo)
    e = jnp.exp(cum)
    einv = jnp.exp(-cum)
    ed = jnp.exp(tot - cum)
    at_s[...] = (aa * jnp.exp(cum - lw)).astype(BF16)
    rt_s[...] = (r * e).astype(BF16)
    bt_s[...] = (bb * einv).astype(BF16)
    kt_s[...] = (kmod * einv).astype(BF16)
    bp_s[...] = (bb * ed).astype(BF16)
    kp_s[...] = (kmod * ed).astype(BF16)
    wl_s[...] = jnp.exp(tot)
    v_s[...] = v
    bon_s[...] = r * kmod * rk_ref[...]

    lane = lax.broadcasted_iota(jnp.int32, (1, LANES), 1)
    m0 = lane < HS_A
    gr = lax.broadcasted_iota(jnp.int32, (4 * cl, 2 * cl), 0)
    gc = lax.broadcasted_iota(jnp.int32, (4 * cl, 2 * cl), 1)
    grb = gr % cl
    gcb = gc % cl
    mask_g = gcb <= jnp.where(gr < 2 * cl, grb - 1, grb)
    is_k = lax.broadcasted_iota(jnp.int32, (cl, 2 * cl), 1) >= cl
    sr = lax.broadcasted_iota(jnp.int32, (LANES, LANES), 0)
    sc = lax.broadcasted_iota(jnp.int32, (LANES, LANES), 1)
    mask_s = (sr // HS_A) == (sc // HS_A)
    zl = jnp.zeros((cl, LANES), F32)

    npair = D_A // LANES

    def chunk(c, carry):
        rows = pl.ds(pl.multiple_of(c * cl, cl), cl)
        row0 = pl.ds(pl.multiple_of(c * cl, cl), 1)
        loaded = []
        for p in range(npair):
            lp = slice(p * LANES, (p + 1) * LANES)
            loaded.append((at_s[rows, lp], rt_s[rows, lp], v_s[rows, lp], bt_s[rows, lp], kt_s[rows, lp],
                           bp_s[rows, lp], kp_s[rows, lp], wl_s[row0, lp], st_s[p]))
        gs, pss, vms = [], [], []
        for p in range(npair):
            at, rt, vv, bt, kt = loaded[p][0:5]
            zb = jnp.zeros_like(at)
            lhs4 = jnp.concatenate([jnp.where(m0, at, zb), jnp.where(m0, zb, at),
                                    jnp.where(m0, rt, zb), jnp.where(m0, zb, rt)], axis=0)
            rhs = jnp.concatenate([bt, kt], axis=0)
            g = lax.dot_general(lhs4, rhs, (((1,), (1,)), ((), ())), preferred_element_type=F32)
            gs.append(jnp.where(mask_g, g, 0.0))
            st = loaded[p][8]
            pss.append(lax.dot_general(jnp.concatenate([at, rt], axis=0), st.astype(BF16),
                                       (((1,), (1,)), ((), ())), preferred_element_type=F32))
            vm0 = jnp.where(m0, vv, 0.0)
            vms.append((vm0, vv - vm0))
        pps = []
        for p in range(npair):
            g, ps, (vm0, vm1) = gs[p], pss[p], vms[p]
            gp0 = jnp.where(is_k, g[0:cl], 0.0)
            gp1 = jnp.where(is_k, g[cl:2 * cl], 0.0)
            pps.append(ps[0:cl] + _dot1(gp0, jnp.concatenate([zl, vm0], axis=0))
                       + _dot1(gp1, jnp.concatenate([zl, vm1], axis=0)))
        ublocks = [[] for _ in range(npair)]
        for kb in range(cl // SUBLANES):
            r0 = kb * SUBLANES
            for p in range(npair):
                g = gs[p]
                d0 = g[r0:r0 + SUBLANES]
                d1 = g[cl + r0:cl + r0 + SUBLANES]
                uk = pps[p][r0:r0 + SUBLANES]
                if kb > 0:
                    ucur = jnp.concatenate(ublocks[p] + [jnp.zeros((cl - r0, LANES), F32)], axis=0)
                    uc0 = jnp.where(m0, ucur, 0.0)
                    uk = uk + _dot1(d0[:, 0:cl], uc0) + _dot1(d1[:, 0:cl], ucur - uc0)
                for s in range(SUBLANES - 1):
                    coef = jnp.where(m0, d0[:, r0 + s:r0 + s + 1], d1[:, r0 + s:r0 + s + 1])
                    uk = uk + coef * uk[s:s + 1, :]
                ublocks[p].append(uk)
        results = []
        for p in range(npair):
            g, ps, (vm0, vm1) = gs[p], pss[p], vms[p]
            vv, bp, kp, wl, st = loaded[p][2], loaded[p][5], loaded[p][6], loaded[p][7], loaded[p][8]
            u = jnp.concatenate(ublocks[p], axis=0)
            um0 = jnp.where(m0, u, 0.0)
            um1 = u - um0
            y = (ps[cl:2 * cl] + _dot1(g[2 * cl:3 * cl], jnp.concatenate([um0, vm0], axis=0))
                 + _dot1(g[3 * cl:4 * cl], jnp.concatenate([um1, vm1], axis=0)))
            uv = jnp.concatenate([u, vv], axis=0).astype(BF16)
            bk = jnp.concatenate([bp, kp], axis=0)
            upd = lax.dot_general(uv, bk, (((0,), (0,)), ((), ())), preferred_element_type=F32)
            results.append((y, jnp.where(mask_s, st * wl + upd, 0.0)))
        for p in range(npair):
            y, st_new = results[p]
            y_s[rows, p * LANES:(p + 1) * LANES] = y
            st_s[p] = st_new
        return carry

    lax.fori_loop(0, tb // cl, chunk, 0)

    oa_ref[...] = _rwkv_out(y_s[...], bon_s[...], v_s[...], ga_ref[...], lng_ref[...], lnb_ref[...], bd, False)
    s_ref[...] = st_s[...]


def _block_mats(tb, cl):
    ri = jnp.arange(tb)[:, None]
    ci = jnp.arange(tb)[None, :]
    same = (ri // cl) == (ci // cl)
    trin = (same & (ci <= ri)).astype(BF16)
    tot = same.astype(BF16)
    return trin, tot


def _bd256():
    i = jnp.arange(256)
    return ((i[:, None] // HS_A) == (i[None, :] // HS_A)).astype(BF16)


def _rwkv_prompt(p3, l, W):
    b, t, _ = p3.shape
    tb, cl = RWKV_TB, RWKV_CHUNK
    assert t % tb == 0
    trin, tot = _block_mats(tb, cl)
    trin_l, tot_l = trin, tot
    row = lambda a: a.reshape(1, -1)
    const = lambda shape: pl.BlockSpec(shape, lambda i, j: (0,) * len(shape))
    in_specs = [pl.BlockSpec((None, tb, W_SHIFT), lambda i, j: (i, j, 0)),
                pl.BlockSpec((None, tb, D_A), lambda i, j: (i, j, BLK_GA)),
                const((1, W_SHIFT)), const((1, D_A)), const((R_W + R_A, D_A)), const((1, D_A)),
                const((R_W + R_A, D_A)), const((1, D_A)), const((1, D_A)), const((1, D_A)),
                const((1, D_A)), const((1, D_A)), const((256, 256)), const((tb, tb)), const((tb, tb))]
    out_specs = [pl.BlockSpec((None, tb, D_A), lambda i, j: (i, j, 0)),
                 pl.BlockSpec((None, D_A // LANES, LANES, LANES), lambda i, j: (i, 0, 0, 0))]
    out_shape = [jax.ShapeDtypeStruct((b, t, D_A), F32),
                 jax.ShapeDtypeStruct((b, D_A // LANES, LANES, LANES), F32)]
    scratch = [pltpu.VMEM((1, W_SHIFT), F32), pltpu.VMEM((D_A // LANES, LANES, LANES), F32)]
    scratch += [pltpu.VMEM((tb, D_A), BF16)] * 6
    scratch += [pltpu.VMEM((tb, D_A), F32)] * 4
    oa, sbd = pl.pallas_call(
        functools.partial(_rwkv_prompt_kernel, tb=tb, cl=cl),
        grid=(b, t // tb), in_specs=in_specs, out_specs=out_specs, out_shape=out_shape,
        scratch_shapes=scratch, compiler_params=_cparams(("arbitrary", "arbitrary")), name="rwkv_prompt",
    )(p3, p3, row(W['rwkv_mu'][l]), row(W['rwkv_w0'][l]), W['w2p_bf16'][l], row(W['rwkv_a0'][l]),
      W['a2p_bf16'][l], row(W['rwkv_kk'][l]), row(W['rwkv_ka'][l]), row(W['rwkv_rk'][l]),
      row(W['rwkv_ln_g'][l]), row(W['rwkv_ln_b'][l]), _bd256(), trin_l, tot_l)
    x = sbd.reshape(b, D_A // LANES, 2, HS_A, 2, HS_A)
    s_t = jnp.stack([x[:, :, 0, :, 0, :], x[:, :, 1, :, 1, :]], axis=2).reshape(b, H_A, HS_A, HS_A)
    return oa, s_t


def _rwkv_step_rows_kernel(pa_ref, prev_ref, mu_ref, w0_ref, w2p_ref, a0_ref, a2p_ref, kkp_ref, ka_ref, rk_ref,
                           bd_ref, r_ref, w_ref, k_ref, v_ref, aa_ref, bb_ref, bon_ref):
    pa = pa_ref[...]
    pm = pa + mu_ref[...] * (prev_ref[...] - pa)
    r, kmod, v, aa, bb, lw = _rwkv_rows(pm, w0_ref[...], w2p_ref[...], a0_ref[...], a2p_ref[...],
                                        kkp_ref[...], ka_ref[...], bd_ref[...], _dot3, True)
    r_ref[...] = r
    w_ref[...] = jnp.exp(lw)
    k_ref[...] = kmod
    v_ref[...] = v
    aa_ref[...] = aa
    bb_ref[...] = bb
    bon_ref[...] = r * kmod * rk_ref[...]


def _rwkv_step_state_kernel(s_ref, r_ref, w_ref, k_ref, aa_ref, bb_ref, vc_ref, so_ref, y_ref):
    s = s_ref[...]
    sa = jnp.sum(s * aa_ref[...], axis=-1, keepdims=True)
    sn = s * w_ref[...] + sa * bb_ref[...] + vc_ref[...] * k_ref[...]
    so_ref[...] = sn
    y_ref[...] = jnp.sum(sn * r_ref[...], axis=-1, keepdims=True)


def _rwkv_step_out_kernel(y_ref, bon_ref, v_ref, ga_ref, lng_ref, lnb_ref, bd_ref, o_ref):
    o_ref[...] = _rwkv_out(y_ref[...], bon_ref[...], v_ref[...], ga_ref[...], lng_ref[...], lnb_ref[...],
                           bd_ref[...], True)


def _rwkv_step(ps2, l, W, shift_prev, s0):
    b = ps2.shape[0]
    row = lambda a: a.reshape(1, -1)
    full = lambda shape: pl.BlockSpec(shape, lambda i: (0,) * len(shape))
    vec = jax.ShapeDtypeStruct((b, D_A), F32)
    outs = pl.pallas_call(
        _rwkv_step_rows_kernel, grid=(1,),
        in_specs=[pl.BlockSpec((b, W_SHIFT), lambda i: (0, 0)), full((b, W_SHIFT)), full((1, W_SHIFT)),
                  full((1, D_A)), full((R_W + R_A, D_A)), full((1, D_A)), full((R_W + R_A, D_A)),
                  full((1, D_A)), full((1, D_A)), full((1, D_A)), full((256, 256))],
        out_specs=[full((b, D_A))] * 7, out_shape=[vec] * 7,
        compiler_params=_cparams(("arbitrary",)), name="rwkv_step_rows",
    )(ps2, shift_prev, row(W['rwkv_mu'][l]), row(W['rwkv_w0'][l]), W['w2p'][l], row(W['rwkv_a0'][l]),
      W['a2p'][l], row(W['rwkv_kk'][l]), row(W['rwkv_ka'][l]), row(W['rwkv_rk'][l]), _bd256())
    r, w, k, v, aa, bb, bon = outs
    hrow = lambda a: a.reshape(b, H_A, 1, HS_A)
    rspec = pl.BlockSpec((None, H_A, 1, HS_A), lambda i: (i, 0, 0, 0))
    cspec = pl.BlockSpec((None, H_A, HS_A, 1), lambda i: (i, 0, 0, 0))
    sspec = pl.BlockSpec((None, H_A, HS_A, HS_A), lambda i: (i, 0, 0, 0))
    s_new, y = pl.pallas_call(
        _rwkv_step_state_kernel, grid=(b,),
        in_specs=[sspec, rspec, rspec, rspec, rspec, rspec, cspec],
        out_specs=[sspec, cspec],
        out_shape=[jax.ShapeDtypeStruct((b, H_A, HS_A, HS_A), F32), jax.ShapeDtypeStruct((b, H_A, HS_A, 1), F32)],
        compiler_params=_cparams(("arbitrary",)), name="rwkv_step_state",
    )(s0, hrow(r), hrow(w), hrow(k), hrow(aa), hrow(bb), v.reshape(b, H_A, HS_A, 1))
    ga = ps2[:, BLK_GA * 1024:(BLK_GA + 1) * 1024]
    oa = pl.pallas_call(
        _rwkv_step_out_kernel, grid=(1,),
        in_specs=[full((b, D_A))] * 4 + [full((1, D_A)), full((1, D_A)), full((256, 256))],
        out_specs=full((b, D_A)), out_shape=vec,
        compiler_params=_cparams(("arbitrary",)), name="rwkv_step_out",
    )(y.reshape(b, D_A), bon, v, ga, row(W['rwkv_ln_g'][l]), row(W['rwkv_ln_b'][l]), _bd256())
    return oa, s_new


def _lambda(lq_ref, lam_init):
    lq = lq_ref[...]
    e1 = jnp.sum(lq[0:1] * lq[1:2], axis=-1, keepdims=True)
    e2 = jnp.sum(lq[2:3] * lq[3:4], axis=-1, keepdims=True)
    return jnp.exp(e1) - jnp.exp(e2) + lam_init


def _diffattn_kernel(q_ref, k_ref, v_ref, gb_ref, lq_ref, sub_ref, o_ref, kb_s, vt_s, m_s, l_s, acc_s,
                     *, tq, nblk, nh, lam_init):
    i = pl.program_id(2)
    hl = [slice(h * VD_B, (h + 1) * VD_B) for h in range(nh)]

    @pl.when(i == 0)
    def _():
        for jj in range(nblk):
            kb_s[jj] = k_ref[jj * tq:(jj + 1) * tq, :].astype(BF16)
            for h in range(nh):
                vt_s[h, jj] = v_ref[jj * tq:(jj + 1) * tq, hl[h]].T.astype(BF16)

    lane = lax.broadcasted_iota(jnp.int32, (1, LANES), 1)
    m0 = lane < HD_B
    qsts = []
    for h in range(nh):
        q = q_ref[:, hl[h]] * (HD_B ** -0.5)
        qs = jnp.concatenate([jnp.where(m0, q, 0.0), jnp.where(m0, 0.0, q)], axis=0)
        qsts.append(qs.T.astype(BF16))
    m_s[...] = jnp.full_like(m_s, -jnp.inf)
    l_s[...] = jnp.zeros_like(l_s)
    acc_s[...] = jnp.zeros_like(acc_s)

    def block(j, masked):
        kb = kb_s[j]
        sts = [jnp.dot(kb[:, hl[h]], qsts[h], preferred_element_type=F32) for h in range(nh)]
        if masked:
            rr = lax.broadcasted_iota(jnp.int32, (tq, 2 * tq), 0)
            cc = lax.broadcasted_iota(jnp.int32, (tq, 2 * tq), 1)
            cc = jnp.where(cc >= tq, cc - tq, cc)
            sts = [jnp.where(rr <= cc, st, -jnp.inf) for st in sts]
        alphas, pbs = [], []
        for h in range(nh):
            m_old = m_s[h]
            m_new = jnp.maximum(m_old, jnp.max(sts[h], axis=0, keepdims=True))
            alpha = jnp.exp(m_old - m_new)
            p = jnp.exp(sts[h] - m_new)
            l_s[h] = alpha * l_s[h] + jnp.sum(p, axis=0, keepdims=True)
            m_s[h] = m_new
            alphas.append(alpha)
            pbs.append(p.astype(BF16))
        for h in range(nh):
            vt = vt_s[h, j]
            for c in range(2):
                cs = slice(c * tq, (c + 1) * tq)
                acc_s[h, c] = alphas[h][:, cs] * acc_s[h, c] + jnp.dot(vt, pbs[h][:, cs],
                                                                      preferred_element_type=F32)

    def body(j, carry):
        block(j, False)
        return carry

    lax.fori_loop(0, i, body, 0)
    block(i, True)

    lam = _lambda(lq_ref, lam_init)
    for h in range(nh):
        linv = 1.0 / l_s[h]
        ot = acc_s[h, 0] * linv[:, 0:tq] - lam * (acc_s[h, 1] * linv[:, tq:2 * tq])
        o = _rms_rows(ot.T, sub_ref[...], SUBLN_EPS) * (1.0 - lam_init)
        o_ref[:, hl[h]] = o * _silu(gb_ref[:, hl[h]])


def _diffattn_prompt(p3, kv4, l, W, lam_init):
    b, t, _ = p3.shape
    tq = 256
    assert t % tq == 0
    lq = jnp.stack([W['diff_lq1'][l], W['diff_lk1'][l], W['diff_lq2'][l], W['diff_lk2'][l]])
    nh = DIFF_HEADS_PER_STEP
    hw = nh * VD_B
    qb0 = BLK_Q * 1024 // hw
    gb0 = BLK_GB * 1024 // hw
    return pl.pallas_call(
        functools.partial(_diffattn_kernel, tq=tq, nblk=t // tq, nh=nh, lam_init=lam_init),
        grid=(b, H_B // nh, t // tq),
        in_specs=[pl.BlockSpec((None, tq, hw), lambda bi, h, i: (bi, i, qb0 + h)),
                  pl.BlockSpec((None, None, t, hw), lambda bi, h, i: (0, bi, 0, h)),
                  pl.BlockSpec((None, None, t, hw), lambda bi, h, i: (1, bi, 0, h)),
                  pl.BlockSpec((None, tq, hw), lambda bi, h, i: (bi, i, gb0 + h)),
                  pl.BlockSpec((4, HD_B), lambda bi, h, i: (0, 0)),
                  pl.BlockSpec((1, VD_B), lambda bi, h, i: (0, 0))],
        out_specs=pl.BlockSpec((None, tq, hw), lambda bi, h, i: (bi, i, h)),
        out_shape=jax.ShapeDtypeStruct((b, t, H_B * VD_B), F32),
        scratch_shapes=[pltpu.VMEM((t // tq, tq, hw), BF16), pltpu.VMEM((nh, t // tq, VD_B, tq), BF16),
                        pltpu.VMEM((nh, 1, 2 * tq), F32), pltpu.VMEM((nh, 1, 2 * tq), F32),
                        pltpu.VMEM((nh, 2, VD_B, tq), F32)],
        compiler_params=_cparams(("arbitrary", "arbitrary", "arbitrary")), name="diffattn_prompt",
    )(p3, kv4, kv4, p3, lq, W['diff_subln'][l].reshape(1, VD_B))


def _paged_kernel(pt_ref, q_ref, kn_ref, vn_ref, gb_ref, lq_ref, sub_ref, *rest, pps, lam_init):
    kpages = rest[0:pps]
    vpages = rest[pps:2 * pps]
    o_ref = rest[2 * pps]
    m_s, l_s, acc_s = rest[2 * pps + 1:]
    step = pl.program_id(1)
    nrow = 2 * H_B
    lane = lax.broadcasted_iota(jnp.int32, (1, LANES), 1)
    m0 = lane < HD_B
    q8 = q_ref[...] * (HD_B ** -0.5)
    q16 = jnp.concatenate([jnp.where(m0, q8, 0.0), jnp.where(m0, 0.0, q8)], axis=0)
    npg = PAGE_SIZE * H_B
    rr = lax.broadcasted_iota(jnp.int32, (nrow, npg), 0)
    cc = lax.broadcasted_iota(jnp.int32, (nrow, npg), 1)
    valid = (cc % H_B) == (rr % H_B)

    @pl.when(step == 0)
    def _():
        m_s[...] = jnp.full_like(m_s, -jnp.inf)
        l_s[...] = jnp.zeros_like(l_s)
        acc_s[...] = jnp.zeros_like(acc_s)

    qb = q16.astype(BF16)
    parts = []
    for kp in kpages:
        kf = kp[...].reshape(npg, VD_B).astype(BF16)
        sc = lax.dot_general(qb, kf, (((1,), (1,)), ((), ())), preferred_element_type=F32)
        parts.append(jnp.where(valid, sc, -jnp.inf))
    s = jnp.concatenate(parts, axis=-1)
    m_old = m_s[...]
    m_new = jnp.maximum(m_old, jnp.max(s, axis=-1, keepdims=True))
    alpha = jnp.exp(m_old - m_new)
    p = jnp.exp(s - m_new)
    l_s[...] = alpha * l_s[...] + jnp.sum(p, axis=-1, keepdims=True)
    m_s[...] = m_new
    pb = p.astype(BF16)
    acc = alpha * acc_s[...]
    for u, vp in enumerate(vpages):
        vf = vp[...].reshape(npg, VD_B).astype(BF16)
        acc = acc + jnp.dot(pb[:, u * npg:(u + 1) * npg], vf, preferred_element_type=F32)
    acc_s[...] = acc

    @pl.when(step == pl.num_programs(1) - 1)
    def _():
        kn = kn_ref[...]
        vn = vn_ref[...]
        s_new = jnp.sum(q16 * jnp.concatenate([kn, kn], axis=0), axis=-1, keepdims=True)
        m_o = m_s[...]
        m_f = jnp.maximum(m_o, s_new)
        a_f = jnp.exp(m_o - m_f)
        p_new = jnp.exp(s_new - m_f)
        l_f = a_f * l_s[...] + p_new
        acc_f = a_f * acc_s[...] + p_new * jnp.concatenate([vn, vn], axis=0)
        o16 = acc_f / l_f
        o = o16[0:H_B] - _lambda(lq_ref, lam_init) * o16[H_B:2 * H_B]
        o = _rms_rows(o, sub_ref[...], SUBLN_EPS) * (1.0 - lam_init)
        o_ref[...] = o * _silu(gb_ref[...])


def _paged_attn(q3, kn3, vn3, gb3, cache_k, cache_v, page_table, l, W, lam_init):
    b = q3.shape[0]
    n_pages = page_table.shape[1]
    pps = PAGES_PER_STEP
    while n_pages % pps:
        pps //= 2
    lq = jnp.stack([W['diff_lq1'][l], W['diff_lk1'][l], W['diff_lq2'][l], W['diff_lk2'][l]])
    tok = pl.BlockSpec((None, H_B, VD_B), lambda bi, s, pt: (bi, 0, 0))

    def page_spec(u):
        return pl.BlockSpec((None, None, PAGE_SIZE, H_B, VD_B),
                            lambda bi, s, pt: (l, pt[bi, s * pps + u], 0, 0, 0))

    grid_spec = pltpu.PrefetchScalarGridSpec(
        num_scalar_prefetch=1, grid=(b, n_pages // pps),
        in_specs=[tok, tok, tok, tok,
                  pl.BlockSpec((4, HD_B), lambda bi, s, pt: (0, 0)),
                  pl.BlockSpec((1, VD_B), lambda bi, s, pt: (0, 0))]
                 + [page_spec(u) for u in range(pps)] + [page_spec(u) for u in range(pps)],
        out_specs=tok,
        scratch_shapes=[pltpu.VMEM((2 * H_B, 1), F32), pltpu.VMEM((2 * H_B, 1), F32),
                        pltpu.VMEM((2 * H_B, VD_B), F32)])
    return pl.pallas_call(
        functools.partial(_paged_kernel, pps=pps, lam_init=lam_init),
        grid_spec=grid_spec, out_shape=jax.ShapeDtypeStruct((b, H_B, VD_B), F32),
        compiler_params=_cparams(("arbitrary", "arbitrary")), name="paged_diffattn",
    )(page_table, q3, kn3, vn3, gb3, lq, W['diff_subln'][l].reshape(1, VD_B),
      *([cache_k] * pps), *([cache_v] * pps))


def _lru_gates(xconv, wa_ref, ba, wx_ref, bx, lam, dotf):
    pr, pi = [], []
    for n in range(NB_C):
        xb = xconv[:, n * BS_C:(n + 1) * BS_C]
        pr.append(dotf(xb, wa_ref[n]))
        pi.append(dotf(xb, wx_ref[n]))
    gate_r = _sigmoid(jnp.concatenate(pr, axis=-1) + ba)
    gate_i = _sigmoid(jnp.concatenate(pi, axis=-1) + bx)
    log_a = -LRU_C * gate_r * _softplus(-lam)
    a = jnp.exp(log_a)
    bv = jnp.sqrt(1.0 - jnp.exp(2.0 * log_a)) * (gate_i * xconv)
    return a, bv


def _lru_prompt_kernel(xc_ref, gc_ref, cp_ref, h0_ref, cw_ref, cb_ref, wa_ref, ba_ref, wx_ref, bx_ref, lam_ref,
                       oc_ref, h_ref, ext_s, hc_s, *, tt):
    t = pl.program_id(1)

    @pl.when(t == 0)
    def _():
        ext_s[0:SUBLANES, :] = cp_ref[...]
        hc_s[...] = h0_ref[...]

    xc = xc_ref[...]
    ext_s[SUBLANES:SUBLANES + tt, :] = xc
    cw = cw_ref[...]
    xconv = cw[3:4] * xc + cb_ref[...]
    for j in range(1, CONV_W):
        xconv = xconv + cw[3 - j:4 - j] * ext_s[pl.ds(SUBLANES - j, tt), :]
    ext_s[0:SUBLANES, :] = xc[tt - SUBLANES:tt, :]

    a, bv = _lru_gates(xconv, wa_ref, ba_ref[...], wx_ref, bx_ref[...], lam_ref[...], _dot1)
    rowi = lax.broadcasted_iota(jnp.int32, (tt, 1), 0)
    s = 1
    while s < tt:
        keep = rowi >= s
        a_sh = jnp.where(keep, pltpu.roll(a, s, 0), 1.0)
        b_sh = jnp.where(keep, pltpu.roll(bv, s, 0), 0.0)
        bv = a * b_sh + bv
        a = a * a_sh
        s *= 2
    h = a * hc_s[...] + bv
    hc_s[...] = h[tt - 1:tt, :]
    h_ref[...] = h[tt - 1:tt, :]
    oc_ref[...] = h * _silu(gc_ref[...])


def _lru_prompt(p3, l, W, conv_prev8, h0):
    b, t, _ = p3.shape
    tt = 256
    assert t % tt == 0
    row = lambda a: a.reshape(1, -1)
    const = lambda shape: pl.BlockSpec(shape, lambda i, j: (0,) * len(shape))
    oc, h = pl.pallas_call(
        functools.partial(_lru_prompt_kernel, tt=tt),
        grid=(b, t // tt),
        in_specs=[pl.BlockSpec((None, tt, D_C), lambda i, j: (i, j, BLK_XC)),
                  pl.BlockSpec((None, tt, D_C), lambda i, j: (i, j, BLK_GC)),
                  pl.BlockSpec((None, SUBLANES, D_C), lambda i, j: (i, 0, 0)),
                  pl.BlockSpec((None, 1, D_C), lambda i, j: (i, 0, 0)),
                  const((CONV_W, D_C)), const((1, D_C)), const((NB_C, BS_C, BS_C)), const((1, D_C)),
                  const((NB_C, BS_C, BS_C)), const((1, D_C)), const((1, D_C))],
        out_specs=[pl.BlockSpec((None, tt, D_C), lambda i, j: (i, j, 0)),
                   pl.BlockSpec((None, 1, D_C), lambda i, j: (i, 0, 0))],
        out_shape=[jax.ShapeDtypeStruct((b, t, D_C), F32), jax.ShapeDtypeStruct((b, 1, D_C), F32)],
        scratch_shapes=[pltpu.VMEM((tt + SUBLANES, D_C), F32), pltpu.VMEM((1, D_C), F32)],
        compiler_params=_cparams(("arbitrary", "arbitrary")), name="lru_prompt",
    )(p3, p3, conv_prev8, h0, W['lru_conv_w'][l], row(W['lru_conv_b'][l]), W['wa_bf16'][l], row(W['lru_ba'][l]),
      W['wx_bf16'][l], row(W['lru_bx'][l]), row(W['lru_lambda'][l]))
    return oc, h.reshape(b, D_C)


def _lru_step_kernel(xc_ref, gc_ref, c0_ref, c1_ref, c2_ref, h0_ref, cw_ref, cb_ref, wa_ref, ba_ref, wx_ref,
                     bx_ref, lam_ref, oc_ref, h_ref):
    cw = cw_ref[...]
    xconv = (cw[0:1] * c0_ref[...] + cw[1:2] * c1_ref[...] + cw[2:3] * c2_ref[...] + cw[3:4] * xc_ref[...]
             + cb_ref[...])
    a, bv = _lru_gates(xconv, wa_ref, ba_ref[...], wx_ref, bx_ref[...], lam_ref[...], _dot3)
    h = a * h0_ref[...] + bv
    h_ref[...] = h
    oc_ref[...] = h * _silu(gc_ref[...])


def _lru_step(ps2, l, W, conv_prev, h0):
    b = ps2.shape[0]
    row = lambda a: a.reshape(1, -1)
    full = lambda shape: pl.BlockSpec(shape, lambda i: (0,) * len(shape))
    xc = ps2[:, BLK_XC * 1024:(BLK_XC + 1) * 1024]
    gc = ps2[:, BLK_GC * 1024:(BLK_GC + 1) * 1024]
    vec = jax.ShapeDtypeStruct((b, D_C), F32)
    oc, h = pl.pallas_call(
        _lru_step_kernel, grid=(1,),
        in_specs=[full((b, D_C))] * 6 + [full((CONV_W, D_C)), full((1, D_C)), full((NB_C, BS_C, BS_C)),
                                          full((1, D_C)), full((NB_C, BS_C, BS_C)), full((1, D_C)), full((1, D_C))],
        out_specs=[full((b, D_C))] * 2, out_shape=[vec, vec],
        compiler_params=_cparams(("arbitrary",)), name="lru_step",
    )(xc, gc, conv_prev[:, 0], conv_prev[:, 1], conv_prev[:, 2], h0, W['lru_conv_w'][l], row(W['lru_conv_b'][l]),
      W['lru_wa'][l], row(W['lru_ba'][l]), W['lru_wx'][l], row(W['lru_bx'][l]), row(W['lru_lambda'][l]))
    conv_new = jnp.stack([conv_prev[:, 1], conv_prev[:, 2], xc], axis=1)
    return oc, h, conv_new


def _merge_kernel(x_ref, oa_ref, ob_ref, oc_ref, g_ref, wb_ref, wo_ref, nm_ref, wq_ref, x1_ref, q_ref):
    acc = None
    for n, o_ref in enumerate((oa_ref, ob_ref, oc_ref)):
        tproj = jnp.dot(o_ref[...].astype(BF16), wb_ref[n], preferred_element_type=F32)
        term = _sigmoid(g_ref[:, n * D_MODEL:(n + 1) * D_MODEL]) * tproj
        acc = term if acc is None else acc + term
    x1 = x_ref[...] + jnp.dot(acc.astype(BF16), wo_ref[...], preferred_element_type=F32)
    x1_ref[...] = x1
    xn = _rms_rows(x1, nm_ref[...], RMS_EPS)
    q_ref[...] = jnp.dot(xn.astype(BF16), wq_ref[...], preferred_element_type=F32)


def _merge(x2, oa2, ob2, oc2, p2, l, W, tm):
    m = x2.shape[0]
    assert m % tm == 0
    tok = pl.BlockSpec((tm, D_MODEL), lambda i: (i, 0))
    const = lambda shape: pl.BlockSpec(shape, lambda i: (0,) * len(shape))
    out = jax.ShapeDtypeStruct((m, D_MODEL), F32)
    return pl.pallas_call(
        _merge_kernel, grid=(m // tm,),
        in_specs=[tok, tok, tok, tok, pl.BlockSpec((tm, 3 * D_MODEL), lambda i: (i, BLK_GTS // 3)),
                  const((3, D_MODEL, D_MODEL)), const((D_MODEL, D_MODEL)), const((1, D_MODEL)),
                  const((D_MODEL, D_MODEL))],
        out_specs=[tok, tok], out_shape=[out, out],
        compiler_params=_cparams(("arbitrary",)), name="merge_outproj",
    )(x2, oa2, ob2, oc2, p2, W['wb_bf16'][l], W['wout_bf16'][l], W['norm_mem'][l].reshape(1, D_MODEL),
      W['wq_bf16'][l])


def _memattn_kernel(q_ref, mk_ref, mv_ref, o_ref):
    outs = []
    for h in range(H_MEM):
        sl = slice(h * HD_MEM, (h + 1) * HD_MEM)
        s = lax.dot_general(q_ref[:, sl].astype(BF16), mk_ref[:, sl].astype(BF16), (((1,), (1,)), ((), ())),
                            preferred_element_type=F32) * (HD_MEM ** -0.5)
        m = jnp.max(s, axis=-1, keepdims=True)
        p = jnp.exp(s - m)
        p = p / jnp.sum(p, axis=-1, keepdims=True)
        outs.append(jnp.dot(p.astype(BF16), mv_ref[:, sl].astype(BF16), preferred_element_type=F32))
    o_ref[...] = jnp.concatenate(outs, axis=-1)


def _memattn(q3, mk3, mv3, tq):
    b, t, _ = q3.shape
    assert t % tq == 0
    tok = pl.BlockSpec((None, tq, D_MODEL), lambda i, j: (i, j, 0))
    mem = pl.BlockSpec((None, N_MEM, D_MODEL), lambda i, j: (i, 0, 0))
    return pl.pallas_call(
        _memattn_kernel, grid=(b, t // tq), in_specs=[tok, mem, mem], out_specs=tok,
        out_shape=jax.ShapeDtypeStruct((b, t, D_MODEL), F32),
        compiler_params=_cparams(("arbitrary", "arbitrary")), name="mem_attn",
    )(q3, mk3, mv3)


def _oproj_kernel(x_ref, o_ref, w_ref, g_ref, y_ref, *, final):
    x2 = x_ref[...] + jnp.dot(o_ref[...].astype(BF16), w_ref[...], preferred_element_type=F32)
    y_ref[...] = _rms_rows(x2, g_ref[...], RMS_EPS) if final else x2


def _oproj(x2, o2, w_bf16, g, final, tm):
    m = x2.shape[0]
    assert m % tm == 0
    tok = pl.BlockSpec((tm, D_MODEL), lambda i: (i, 0))
    return pl.pallas_call(
        functools.partial(_oproj_kernel, final=final), grid=(m // tm,),
        in_specs=[tok, tok, pl.BlockSpec((D_MODEL, D_MODEL), lambda i: (0, 0)),
                  pl.BlockSpec((1, D_MODEL), lambda i: (0, 0))],
        out_specs=tok, out_shape=jax.ShapeDtypeStruct((m, D_MODEL), F32),
        compiler_params=_cparams(("arbitrary",)), name="mem_outproj",
    )(x2, o2, w_bf16, g.reshape(1, D_MODEL))


def _prep_weights(W):
    w_in = W['w_in']
    depth = w_in.shape[0]
    offs = [0]
    for s in (W_SHIFT, D_A, 1024, 1024, D_MODEL, D_MODEL, D_C, D_C, 3 * D_MODEL):
        offs.append(offs[-1] + s)
    seg = lambda n: w_in[:, :, offs[n]:offs[n + 1]]
    pa, ga, qb, kb, vb, gb, xc, gc, gts = (seg(n) for n in range(9))
    pad = jnp.zeros((depth, D_MODEL, PA_PAD - W_SHIFT), w_in.dtype)
    wp = jnp.concatenate([pa, pad, ga, qb, gb, xc, gc, gts], axis=-1)
    wkv = jnp.concatenate([kb, vb], axis=-1)
    z = jnp.zeros((depth, R_W, D_A), F32)
    out = dict(W)
    out['wp'], out['wkv'] = wp, wkv
    out['wp_bf16'], out['wkv_bf16'] = wp.astype(BF16), wkv.astype(BF16)
    out['w2p'] = jnp.concatenate([W['rwkv_w2'], z], axis=1)
    out['a2p'] = jnp.concatenate([z, W['rwkv_a2']], axis=1)
    out['w2p_bf16'], out['a2p_bf16'] = out['w2p'].astype(BF16), out['a2p'].astype(BF16)
    out['wa_bf16'], out['wx_bf16'] = W['lru_wa'].astype(BF16), W['lru_wx'].astype(BF16)
    out['wb_bf16'] = W['w_branch'].astype(BF16)
    out['wout_bf16'] = W['w_out'].astype(BF16)
    out['wq_bf16'] = W['w_mem_q'].astype(BF16)
    out['wmkv_bf16'] = W['w_mem_kv'].astype(BF16)
    out['wmo_bf16'] = W['w_mem_o'].astype(BF16)
    return out


def _prompt_layer(W, l, x3, mem2, last):
    b, t, _ = x3.shape
    x2 = x3.reshape(b * t, D_MODEL)
    lam_init = 0.8 - 0.6 * math.exp(-0.3 * l)
    mkv = _rmsmm(mem2, W['norm_memkv'][l], W['wmkv_bf16'][l], tm=min(1024, mem2.shape[0]), tn=D_MODEL, nsplit=2)
    tm = min(1024, b * t)
    p2 = _rmsmm(x2, W['norm_mix'][l], W['wp_bf16'][l], tm=tm, tn=2048)
    kv = _rmsmm(x2, W['norm_mix'][l], W['wkv_bf16'][l], tm=tm, tn=D_MODEL, nsplit=2)
    p3 = p2.reshape(b, t, N_PROJ)
    kv4 = kv.reshape(2, b, t, D_MODEL)
    oa, s_t = _rwkv_prompt(p3, l, W)
    ob = _diffattn_prompt(p3, kv4, l, W, lam_init)
    oc, h_t = _lru_prompt(p3, l, W, jnp.zeros((b, SUBLANES, D_C), F32), jnp.zeros((b, 1, D_C), F32))
    tmm = min(256, b * t)
    x1, q = _merge(x2, oa.reshape(b * t, D_A), ob.reshape(b * t, D_MODEL), oc.reshape(b * t, D_C), p2, l, W, tmm)
    mk3 = mkv[0].reshape(b, N_MEM, D_MODEL)
    mv3 = mkv[1].reshape(b, N_MEM, D_MODEL)
    o = _memattn(q.reshape(b, t, D_MODEL), mk3, mv3, min(512, t))
    g = W['norm_final'] if last else W['norm_mem'][l]
    xo = _oproj(x1, o.reshape(b * t, D_MODEL), W['wmo_bf16'][l], g, last, min(512, b * t))
    k_l = kv4[0].reshape(b, t, H_B, 2 * HD_B)
    v_l = kv4[1].reshape(b, t, H_B, VD_B)
    mk_l = mk3.reshape(b, N_MEM, H_MEM, HD_MEM)
    mv_l = mv3.reshape(b, N_MEM, H_MEM, HD_MEM)
    shift = p3[:, t - 1, 0:W_SHIFT]
    conv = p3[:, t - (CONV_W - 1):, BLK_XC * 1024:(BLK_XC + 1) * 1024]
    return xo.reshape(b, t, D_MODEL), (k_l, v_l, mk_l, mv_l, s_t, shift, h_t, conv)


def _sample_layer(W, l, xs3, shift_prev, s0, conv_prev, h0, mem_k, mem_v, cache_k4, cache_v4, page_table, last):
    b = xs3.shape[0]
    x2 = xs3.reshape(b, D_MODEL)
    lam_init = 0.8 - 0.6 * math.exp(-0.3 * l)
    p2 = _rmsmm(x2, W['norm_mix'][l], W['wp'][l], tm=b, tn=1024, precise=True)
    kv = _rmsmm(x2, W['norm_mix'][l], W['wkv'][l], tm=b, tn=D_MODEL, nsplit=2, precise=True)
    oa, s_new = _rwkv_step(p2, l, W, shift_prev, s0)
    q3 = p2[:, BLK_Q * 1024:(BLK_Q + 1) * 1024].reshape(b, H_B, VD_B)
    gb3 = p2[:, BLK_GB * 1024:(BLK_GB + 1) * 1024].reshape(b, H_B, VD_B)
    ob = _paged_attn(q3, kv[0].reshape(b, H_B, VD_B), kv[1].reshape(b, H_B, VD_B), gb3, cache_k4, cache_v4,
                     page_table, l, W, lam_init)
    oc, h_new, conv_new = _lru_step(p2, l, W, conv_prev, h0)
    x1, q = _merge(x2, oa, ob.reshape(b, D_MODEL), oc, p2, l, W, b)
    o = _memattn(q.reshape(b, 1, D_MODEL), mem_k.reshape(b, N_MEM, D_MODEL), mem_v.reshape(b, N_MEM, D_MODEL), 1)
    g = W['norm_final'] if last else W['norm_mem'][l]
    xo = _oproj(x1, o.reshape(b, D_MODEL), W['wmo_bf16'][l], g, last, b)
    k_l = kv[0].reshape(b, 1, H_B, 2 * HD_B)
    v_l = kv[1].reshape(b, 1, H_B, VD_B)
    shift = p2[:, 0:W_SHIFT]
    return xo.reshape(b, 1, D_MODEL), (k_l, v_l, s_new, shift, h_new, conv_new)


def kernel(x_prompt, x_sample, cache_diff_k, cache_diff_v, cache_mem_k, cache_mem_v, state_rwkv_S, state_rwkv_shift, state_lru_h, state_lru_conv, page_table, mem_prompt, norm_mix, w_in, rwkv_mu, rwkv_w0, rwkv_w2, rwkv_a0, rwkv_a2, rwkv_kk, rwkv_ka, rwkv_rk, rwkv_ln_g, rwkv_ln_b, diff_lq1, diff_lk1, diff_lq2, diff_lk2, diff_subln, lru_conv_w, lru_conv_b, lru_wa, lru_ba, lru_wx, lru_bx, lru_lambda, w_branch, w_out, norm_mem, norm_memkv, w_mem_q, w_mem_kv, w_mem_o, norm_final):
    W = _prep_weights(dict(
        norm_mix=norm_mix, w_in=w_in, rwkv_mu=rwkv_mu, rwkv_w0=rwkv_w0, rwkv_w2=rwkv_w2, rwkv_a0=rwkv_a0,
        rwkv_a2=rwkv_a2, rwkv_kk=rwkv_kk, rwkv_ka=rwkv_ka, rwkv_rk=rwkv_rk, rwkv_ln_g=rwkv_ln_g,
        rwkv_ln_b=rwkv_ln_b, diff_lq1=diff_lq1, diff_lk1=diff_lk1, diff_lq2=diff_lq2, diff_lk2=diff_lk2,
        diff_subln=diff_subln, lru_conv_w=lru_conv_w, lru_conv_b=lru_conv_b, lru_wa=lru_wa, lru_ba=lru_ba,
        lru_wx=lru_wx, lru_bx=lru_bx, lru_lambda=lru_lambda, w_branch=w_branch, w_out=w_out, norm_mem=norm_mem,
        norm_memkv=norm_memkv, w_mem_q=w_mem_q, w_mem_kv=w_mem_kv, w_mem_o=w_mem_o, norm_final=norm_final))
    depth = w_in.shape[0]
    bp = x_prompt.shape[0]
    mem2 = mem_prompt.reshape(bp * N_MEM, D_MODEL)

    xp = x_prompt
    p_outs = []
    for l in range(depth):
        xp, outs = _prompt_layer(W, l, xp, mem2, l == depth - 1)
        p_outs.append(outs)

    ck4, cv4 = cache_diff_k, cache_diff_v
    xs = x_sample
    s_outs = []
    for l in range(depth):
        xs, outs = _sample_layer(W, l, xs, state_rwkv_shift[l], state_rwkv_S[l], state_lru_conv[l], state_lru_h[l],
                                 cache_mem_k[l], cache_mem_v[l], ck4, cv4, page_table, l == depth - 1)
        s_outs.append(outs)

    stack = lambda outs, n: jnp.stack([o[n] for o in outs])
    return (xp, xs,
            stack(p_outs, 0), stack(p_outs, 1), stack(p_outs, 2), stack(p_outs, 3),
            stack(p_outs, 4), stack(p_outs, 5), stack(p_outs, 6), stack(p_outs, 7),
            stack(s_outs, 0), stack(s_outs, 1), stack(s_outs, 2), stack(s_outs, 3),
            stack(s_outs, 4), stack(s_outs, 5))
```

```python
import functools
import math

import jax
import jax.numpy as jnp
from jax import lax
from jax.experimental import pallas as pl
from jax.experimental.pallas import tpu as pltpu

F32 = jnp.float32
BF16 = jnp.bfloat16

D_MODEL = 1024
HS_A = 64
H_A = D_MODEL // HS_A
D_A = H_A * HS_A
R_W = 64
R_A = 64
LN_X_EPS = 64e-5
HD_B = 64
H_B = D_MODEL // (2 * HD_B)
VD_B = 2 * HD_B
SUBLN_EPS = 1e-5
D_C = D_MODEL
NB_C = 4
BS_C = D_C // NB_C
CONV_W = 4
LRU_C = 8.0
N_MEM = 256
H_MEM = 4
HD_MEM = D_MODEL // H_MEM
PAGE_SIZE = 128
W_SHIFT = 3 * D_A + R_W + R_A
RMS_EPS = 1e-6

LANES = 128
SUBLANES = 8
VMEM_LIMIT_BYTES = 56 * 1024 * 1024

PA_PAD = 4096
BLK_GA, BLK_Q, BLK_GB, BLK_XC, BLK_GC, BLK_GTS = 4, 5, 6, 7, 8, 9
N_PROJ = 12 * 1024

RWKV_CHUNK = 32
RWKV_TB = 128
PAGES_PER_STEP = 8
DIFF_HEADS_PER_STEP = 4


def _cparams(sem):
    return pltpu.CompilerParams(dimension_semantics=sem, vmem_limit_bytes=VMEM_LIMIT_BYTES)


def _dot1(a, b):
    return jnp.dot(a.astype(BF16), b.astype(BF16), preferred_element_type=F32)


def _split(a):
    hi = a.astype(BF16)
    lo = (a - hi.astype(F32)).astype(BF16)
    return hi, lo


def _dot3(a, b):
    ah, al = _split(a)
    bh, bl = _split(b)
    d = functools.partial(jnp.dot, preferred_element_type=F32)
    return d(ah, bh) + (d(ah, bl) + d(al, bh))


def _dot_exact_rhs(a, b_bf16):
    ah, al = _split(a)
    d = functools.partial(jnp.dot, preferred_element_type=F32)
    return d(ah, b_bf16) + d(al, b_bf16)


def _sigmoid(x):
    return 1.0 / (1.0 + jnp.exp(-x))


def _silu(x):
    return x * _sigmoid(x)


def _softplus(x):
    return jnp.maximum(x, 0.0) + jnp.log1p(jnp.exp(-jnp.abs(x)))


def _rms_rows(x, g, eps):
    ms = jnp.mean(x * x, axis=-1, keepdims=True)
    return x * lax.rsqrt(ms + eps) * g


def _segsum64(x, bd, precise):
    outs = []
    for q in range(x.shape[-1] // 256):
        xs = x[:, q * 256:(q + 1) * 256]
        if precise:
            outs.append(_dot_exact_rhs(xs, bd))
        else:
            outs.append(jnp.dot(xs.astype(BF16), bd, preferred_element_type=F32))
    return jnp.concatenate(outs, axis=-1)


def _rmsmm_kernel(x_ref, g_ref, w_ref, o_ref, xn_ref, *, eps, precise):
    @pl.when(pl.program_id(1) == 0)
    def _():
        xn_ref[...] = _rms_rows(x_ref[...], g_ref[...], eps).astype(xn_ref.dtype)

    if precise:
        o_ref[...] = _dot3(xn_ref[...], w_ref[...])
    else:
        o_ref[...] = jnp.dot(xn_ref[...], w_ref[...], preferred_element_type=F32)


def _rmsmm(x2, g, w, *, tm, tn, nsplit=1, precise=False, eps=RMS_EPS):
    m, d = x2.shape
    n = w.shape[1]
    assert m % tm == 0 and n % tn == 0
    if nsplit == 1:
        out_shape = jax.ShapeDtypeStruct((m, n), F32)
        out_spec = pl.BlockSpec((tm, tn), lambda i, j: (i, j))
    else:
        assert tn * nsplit == n
        out_shape = jax.ShapeDtypeStruct((nsplit, m, tn), F32)
        out_spec = pl.BlockSpec((None, tm, tn), lambda i, j: (j, i, 0))
    return pl.pallas_call(
        functools.partial(_rmsmm_kernel, eps=eps, precise=precise),
        grid=(m // tm, n // tn),
        in_specs=[pl.BlockSpec((tm, d), lambda i, j: (i, 0)),
                  pl.BlockSpec((1, d), lambda i, j: (0, 0)),
                  pl.BlockSpec((d, tn), lambda i, j: (0, j))],
        out_specs=out_spec,
        out_shape=out_shape,
        scratch_shapes=[pltpu.VMEM((tm, d), F32 if precise else BF16)],
        compiler_params=_cparams(("arbitrary", "arbitrary")),
        name="rms_matmul_hp" if precise else "rms_matmul",
    )(x2, g.reshape(1, d), w)


def _kvproj_kernel(x_ref, g_ref, w_ref, *rest, eps):
    k_ref, v_ref, xn_ref = rest[-3:]
    j = pl.program_id(1)

    @pl.when(j == 0)
    def _():
        xn_ref[...] = _rms_rows(x_ref[...], g_ref[...], eps).astype(xn_ref.dtype)

    res = jnp.dot(xn_ref[...], w_ref[...], preferred_element_type=F32)

    @pl.when(j == 0)
    def _():
        k_ref[...] = res

    @pl.when(j == 1)
    def _():
        v_ref[...] = res


def _kvproj(x2, g, wkv, l, depth, prev, *, tm, eps=RMS_EPS):
    m, d = x2.shape
    n = wkv.shape[1] // 2
    assert m % tm == 0
    out_spec = pl.BlockSpec((None, tm, n), lambda i, j: (l, i, 0))
    out_shape = jax.ShapeDtypeStruct((depth, m, n), F32)
    in_specs = [pl.BlockSpec((tm, d), lambda i, j: (i, 0)),
                pl.BlockSpec((1, d), lambda i, j: (0, 0)),
                pl.BlockSpec((d, n), lambda i, j: (0, j))]
    args = [x2, g.reshape(1, d), wkv]
    aliases = {}
    if prev is not None:
        in_specs += [pl.BlockSpec(memory_space=pl.ANY), pl.BlockSpec(memory_space=pl.ANY)]
        args += list(prev)
        aliases = {3: 0, 4: 1}
    return pl.pallas_call(
        functools.partial(_kvproj_kernel, eps=eps),
        grid=(m // tm, 2), in_specs=in_specs, out_specs=[out_spec, out_spec], out_shape=[out_shape, out_shape],
        scratch_shapes=[pltpu.VMEM((tm, d), BF16)], input_output_aliases=aliases,
        compiler_params=_cparams(("arbitrary", "arbitrary")), name="kv_proj",
    )(*args)


def _rwkv_rows(pm, w0, w2p, a0, a2p, kkp, ka, bd, dotf, precise):
    r = pm[:, 0:D_A]
    k = pm[:, D_A:2 * D_A]
    v = pm[:, 2 * D_A:3 * D_A]
    x = pm[:, 3 * D_A:3 * D_A + R_W + R_A]
    lane = lax.broadcasted_iota(jnp.int32, (1, R_W + R_A), 1)
    xw = jnp.where(lane < R_W, jnp.tanh(x), 0.0)
    wpre = w0 + dotf(xw, w2p)
    apre = a0 + dotf(x, a2p)
    lw = -jnp.exp(-_softplus(-wpre) - 0.5)
    a = _sigmoid(apre)
    kk = k * kkp
    nrm = jnp.sqrt(_segsum64(kk * kk, bd, precise))
    kk = kk / jnp.maximum(nrm, 1e-12)
    kmod = k * (1.0 + (a - 1.0) * ka)
    return r, kmod, v, -kk, kk * a, lw


def _rwkv_out(y, bon, v, ga, lng, lnb, bd, precise):
    mean = _segsum64(y, bd, True) * (1.0 / HS_A)
    d = y - mean
    var = _segsum64(d * d, bd, precise) * (1.0 / HS_A)
    yn = d * lax.rsqrt(var + LN_X_EPS) * lng + lnb
    bonus = _segsum64(bon, bd, precise) * v
    return (yn + bonus) * _silu(ga)


def _rwkv_prompt_kernel(pa_ref, ga_ref, mu_ref, w0_ref, w2p_ref, a0_ref, a2p_ref, kkp_ref, ka_ref, rk_ref,
                        lng_ref, lnb_ref, bd_ref, trin_ref, tot_ref,
                        oa_ref, s_ref,
                        prev_s, st_s, at_s, rt_s, bt_s, kt_s, bp_s, kp_s, v_s, wl_s, y_s, bon_s, *, tb, cl):
    t = pl.program_id(1)

    @pl.when(t == 0)
    def _():
        prev_s[...] = jnp.zeros_like(prev_s)
        st_s[...] = jnp.zeros_like(st_s)

    bd = bd_ref[...]
    pa = pa_ref[...]
    rowi = lax.broadcasted_iota(jnp.int32, (tb, 1), 0)
    prev = jnp.where(rowi == 0, prev_s[...], pltpu.roll(pa, 1, 0))
    pm = pa + mu_ref[...] * (prev - pa)
    prev_s[...] = pa[tb - 1:tb, :]

    r, kmod, v, aa, bb, lw = _rwkv_rows(pm, w0_ref[...], w2p_ref[...], a0_ref[...], a2p_ref[...],
                                        kkp_ref[...], ka_ref[...], bd, _dot1, False)
    lw_hi, lw_lo = _split(lw)
    mm = functools.partial(jnp.dot, preferred_element_type=F32)
    cum = mm(trin_ref[...], lw_hi) + mm(trin_ref[...], lw_lo)
    tot = mm(tot_ref[...], lw_hi) + mm(tot_ref[...], lw_lo)
    e = jnp.exp(cum)
    einv = jnp.exp(-cum)
    ed = jnp.exp(tot - cum)
    at_s[...] = (aa * jnp.exp(cum - lw)).astype(BF16)
    rt_s[...] = (r * e).astype(BF16)
    bt_s[...] = (bb * einv).astype(BF16)
    kt_s[...] = (kmod * einv).astype(BF16)
    bp_s[...] = (bb * ed).astype(BF16)
    kp_s[...] = (kmod * ed).astype(BF16)
    wl_s[...] = jnp.exp(tot)
    v_s[...] = v
    bon_s[...] = r * kmod * rk_ref[...]

    lane = lax.broadcasted_iota(jnp.int32, (1, LANES), 1)
    m0 = lane < HS_A
    gr = lax.broadcasted_iota(jnp.int32, (4 * cl, 2 * cl), 0)
    gc = lax.broadcasted_iota(jnp.int32, (4 * cl, 2 * cl), 1)
    grb = gr % cl
    gcb = gc % cl
    mask_g = gcb <= jnp.where(gr < 2 * cl, grb - 1, grb)
    is_k = lax.broadcasted_iota(jnp.int32, (cl, 2 * cl), 1) >= cl
    sr = lax.broadcasted_iota(jnp.int32, (LANES, LANES), 0)
    sc = lax.broadcasted_iota(jnp.int32, (LANES, LANES), 1)
    mask_s = (sr // HS_A) == (sc // HS_A)
    zl = jnp.zeros((cl, LANES), F32)

    npair = D_A // LANES

    def chunk(c, carry):
        rows = pl.ds(pl.multiple_of(c * cl, cl), cl)
        row0 = pl.ds(pl.multiple_of(c * cl, cl), 1)
        loaded = []
        for p in range(npair):
            lp = slice(p * LANES, (p + 1) * LANES)
            loaded.append((at_s[rows, lp], rt_s[rows, lp], v_s[rows, lp], bt_s[rows, lp], kt_s[rows, lp],
                           bp_s[rows, lp], kp_s[rows, lp], wl_s[row0, lp], st_s[p]))
        gs, pss, vms = [], [], []
        for p in range(npair):
            at, rt, vv, bt, kt = loaded[p][0:5]
            zb = jnp.zeros_like(at)
            lhs4 = jnp.concatenate([jnp.where(m0, at, zb), jnp.where(m0, zb, at),
                                    jnp.where(m0, rt, zb), jnp.where(m0, zb, rt)], axis=0)
            rhs = jnp.concatenate([bt, kt], axis=0)
            g = lax.dot_general(lhs4, rhs, (((1,), (1,)), ((), ())), preferred_element_type=F32)
            gs.append(jnp.where(mask_g, g, 0.0))
            st = loaded[p][8]
            pss.append(lax.dot_general(jnp.concatenate([at, rt], axis=0), st.astype(BF16),
                                       (((1,), (1,)), ((), ())), preferred_element_type=F32))
            vm0 = jnp.where(m0, vv, 0.0)
            vms.append((vm0, vv - vm0))
        pps = []
        for p in range(npair):
            g, ps, (vm0, vm1) = gs[p], pss[p], vms[p]
            gp0 = jnp.where(is_k, g[0:cl], 0.0)
            gp1 = jnp.where(is_k, g[cl:2 * cl], 0.0)
            pps.append(ps[0:cl] + _dot1(gp0, jnp.concatenate([zl, vm0], axis=0))
                       + _dot1(gp1, jnp.concatenate([zl, vm1], axis=0)))
        ublocks = [[] for _ in range(npair)]
        for kb in range(cl // SUBLANES):
            r0 = kb * SUBLANES
            for p in range(npair):
                g = gs[p]
                d0 = g[r0:r0 + SUBLANES]
                d1 = g[cl + r0:cl + r0 + SUBLANES]
                uk = pps[p][r0:r0 + SUBLANES]
                if kb > 0:
                    ucur = jnp.concatenate(ublocks[p] + [jnp.zeros((cl - r0, LANES), F32)], axis=0)
                    uc0 = jnp.where(m0, ucur, 0.0)
                    uk = uk + _dot1(d0[:, 0:cl], uc0) + _dot1(d1[:, 0:cl], ucur - uc0)
                for s in range(SUBLANES - 1):
                    coef = jnp.where(m0, d0[:, r0 + s:r0 + s + 1], d1[:, r0 + s:r0 + s + 1])
                    uk = uk + coef * uk[s:s + 1, :]
                ublocks[p].append(uk)
        results = []
        for p in range(npair):
            g, ps, (vm0, vm1) = gs[p], pss[p], vms[p]
            vv, bp, kp, wl, st = loaded[p][2], loaded[p][5], loaded[p][6], loaded[p][7], loaded[p][8]
            u = jnp.concatenate(ublocks[p], axis=0)
            um0 = jnp.where(m0, u, 0.0)
            um1 = u - um0
            y = (ps[cl:2 * cl] + _dot1(g[2 * cl:3 * cl], jnp.concatenate([um0, vm0], axis=0))
                 + _dot1(g[3 * cl:4 * cl], jnp.concatenate([um1, vm1], axis=0)))
            uv = jnp.concatenate([u, vv], axis=0).astype(BF16)
            bk = jnp.concatenate([bp, kp], axis=0)
            upd = lax.dot_general(uv, bk, (((0,), (0,)), ((), ())), preferred_element_type=F32)
            results.append((y, jnp.where(mask_s, st * wl + upd, 0.0)))
        for p in range(npair):
            y, st_new = results[p]
            y_s[rows, p * LANES:(p + 1) * LANES] = y
            st_s[p] = st_new
        return carry

    lax.fori_loop(0, tb // cl, chunk, 0)

    oa_ref[...] = _rwkv_out(y_s[...], bon_s[...], v_s[...], ga_ref[...], lng_ref[...], lnb_ref[...], bd, False)
    s_ref[...] = st_s[...]


def _block_mats(tb, cl):
    ri = jnp.arange(tb)[:, None]
    ci = jnp.arange(tb)[None, :]
    same = (ri // cl) == (ci // cl)
    trin = (same & (ci <= ri)).astype(BF16)
    tot = same.astype(BF16)
    return trin, tot


def _bd256():
    i = jnp.arange(256)
    return ((i[:, None] // HS_A) == (i[None, :] // HS_A)).astype(BF16)


def _rwkv_prompt(p3, l, W):
    b, t, _ = p3.shape
    tb, cl = RWKV_TB, RWKV_CHUNK
    assert t % tb == 0
    trin, tot = _block_mats(tb, cl)
    trin_l, tot_l = trin, tot
    row = lambda a: a.reshape(1, -1)
    const = lambda shape: pl.BlockSpec(shape, lambda i, j: (0,) * len(shape))
    in_specs = [pl.BlockSpec((None, tb, W_SHIFT), lambda i, j: (i, j, 0)),
                pl.BlockSpec((None, tb, D_A), lambda i, j: (i, j, BLK_GA)),
                const((1, W_SHIFT)), const((1, D_A)), const((R_W + R_A, D_A)), const((1, D_A)),
                const((R_W + R_A, D_A)), const((1, D_A)), const((1, D_A)), const((1, D_A)),
                const((1, D_A)), const((1, D_A)), const((256, 256)), const((tb, tb)), const((tb, tb))]
    out_specs = [pl.BlockSpec((None, tb, D_A), lambda i, j: (i, j, 0)),
                 pl.BlockSpec((None, D_A // LANES, LANES, LANES), lambda i, j: (i, 0, 0, 0))]
    out_shape = [jax.ShapeDtypeStruct((b, t, D_A), F32),
                 jax.ShapeDtypeStruct((b, D_A // LANES, LANES, LANES), F32)]
    scratch = [pltpu.VMEM((1, W_SHIFT), F32), pltpu.VMEM((D_A // LANES, LANES, LANES), F32)]
    scratch += [pltpu.VMEM((tb, D_A), BF16)] * 6
    scratch += [pltpu.VMEM((tb, D_A), F32)] * 4
    oa, sbd = pl.pallas_call(
        functools.partial(_rwkv_prompt_kernel, tb=tb, cl=cl),
        grid=(b, t // tb), in_specs=in_specs, out_specs=out_specs, out_shape=out_shape,
        scratch_shapes=scratch, compiler_params=_cparams(("arbitrary", "arbitrary")), name="rwkv_prompt",
    )(p3, p3, row(W['rwkv_mu'][l]), row(W['rwkv_w0'][l]), W['w2p_bf16'][l], row(W['rwkv_a0'][l]),
      W['a2p_bf16'][l], row(W['rwkv_kk'][l]), row(W['rwkv_ka'][l]), row(W['rwkv_rk'][l]),
      row(W['rwkv_ln_g'][l]), row(W['rwkv_ln_b'][l]), _bd256(), trin_l, tot_l)
    x = sbd.reshape(b, D_A // LANES, 2, HS_A, 2, HS_A)
    s_t = jnp.stack([x[:, :, 0, :, 0, :], x[:, :, 1, :, 1, :]], axis=2).reshape(b, H_A, HS_A, HS_A)
    return oa, s_t


def _rwkv_step_rows_kernel(pa_ref, prev_ref, mu_ref, w0_ref, w2p_ref, a0_ref, a2p_ref, kkp_ref, ka_ref, rk_ref,
                           bd_ref, r_ref, w_ref, k_ref, v_ref, aa_ref, bb_ref, bon_ref):
    pa = pa_ref[...]
    pm = pa + mu_ref[...] * (prev_ref[...] - pa)
    r, kmod, v, aa, bb, lw = _rwkv_rows(pm, w0_ref[...], w2p_ref[...], a0_ref[...], a2p_ref[...],
                                        kkp_ref[...], ka_ref[...], bd_ref[...], _dot3, True)
    r_ref[...] = r
    w_ref[...] = jnp.exp(lw)
    k_ref[...] = kmod
    v_ref[...] = v
    aa_ref[...] = aa
    bb_ref[...] = bb
    bon_ref[...] = r * kmod * rk_ref[...]


def _rwkv_step_state_kernel(s_ref, r_ref, w_ref, k_ref, aa_ref, bb_ref, vc_ref, so_ref, y_ref):
    s = s_ref[...]
    sa = jnp.sum(s * aa_ref[...], axis=-1, keepdims=True)
    sn = s * w_ref[...] + sa * bb_ref[...] + vc_ref[...] * k_ref[...]
    so_ref[...] = sn
    y_ref[...] = jnp.sum(sn * r_ref[...], axis=-1, keepdims=True)


def _rwkv_step_out_kernel(y_ref, bon_ref, v_ref, ga_ref, lng_ref, lnb_ref, bd_ref, o_ref):
    o_ref[...] = _rwkv_out(y_ref[...], bon_ref[...], v_ref[...], ga_ref[...], lng_ref[...], lnb_ref[...],
                           bd_ref[...], True)


def _rwkv_step(ps2, l, W, shift_prev, s0):
    b = ps2.shape[0]
    row = lambda a: a.reshape(1, -1)
    full = lambda shape: pl.BlockSpec(shape, lambda i: (0,) * len(shape))
    vec = jax.ShapeDtypeStruct((b, D_A), F32)
    outs = pl.pallas_call(
        _rwkv_step_rows_kernel, grid=(1,),
        in_specs=[pl.BlockSpec((b, W_SHIFT), lambda i: (0, 0)), full((b, W_SHIFT)), full((1, W_SHIFT)),
                  full((1, D_A)), full((R_W + R_A, D_A)), full((1, D_A)), full((R_W + R_A, D_A)),
                  full((1, D_A)), full((1, D_A)), full((1, D_A)), full((256, 256))],
        out_specs=[full((b, D_A))] * 7, out_shape=[vec] * 7,
        compiler_params=_cparams(("arbitrary",)), name="rwkv_step_rows",
    )(ps2, shift_prev, row(W['rwkv_mu'][l]), row(W['rwkv_w0'][l]), W['w2p'][l], row(W['rwkv_a0'][l]),
      W['a2p'][l], row(W['rwkv_kk'][l]), row(W['rwkv_ka'][l]), row(W['rwkv_rk'][l]), _bd256())
    r, w, k, v, aa, bb, bon = outs
    hrow = lambda a: a.reshape(b, H_A, 1, HS_A)
    rspec = pl.BlockSpec((None, H_A, 1, HS_A), lambda i: (i, 0, 0, 0))
    cspec = pl.BlockSpec((None, H_A, HS_A, 1), lambda i: (i, 0, 0, 0))
    sspec = pl.BlockSpec((None, H_A, HS_A, HS_A), lambda i: (i, 0, 0, 0))
    s_new, y = pl.pallas_call(
        _rwkv_step_state_kernel, grid=(b,),
        in_specs=[sspec, rspec, rspec, rspec, rspec, rspec, cspec],
        out_specs=[sspec, cspec],
        out_shape=[jax.ShapeDtypeStruct((b, H_A, HS_A, HS_A), F32), jax.ShapeDtypeStruct((b, H_A, HS_A, 1), F32)],
        compiler_params=_cparams(("arbitrary",)), name="rwkv_step_state",
    )(s0, hrow(r), hrow(w), hrow(k), hrow(aa), hrow(bb), v.reshape(b, H_A, HS_A, 1))
    ga = ps2[:, BLK_GA * 1024:(BLK_GA + 1) * 1024]
    oa = pl.pallas_call(
        _rwkv_step_out_kernel, grid=(1,),
        in_specs=[full((b, D_A))] * 4 + [full((1, D_A)), full((1, D_A)), full((256, 256))],
        out_specs=full((b, D_A)), out_shape=vec,
        compiler_params=_cparams(("arbitrary",)), name="rwkv_step_out",
    )(y.reshape(b, D_A), bon, v, ga, row(W['rwkv_ln_g'][l]), row(W['rwkv_ln_b'][l]), _bd256())
    return oa, s_new


def _lambda(lq_ref, lam_init):
    lq = lq_ref[...]
    e1 = jnp.sum(lq[0:1] * lq[1:2], axis=-1, keepdims=True)
    e2 = jnp.sum(lq[2:3] * lq[3:4], axis=-1, keepdims=True)
    return jnp.exp(e1) - jnp.exp(e2) + lam_init


def _diffattn_kernel(q_ref, k_ref, v_ref, gb_ref, lq_ref, sub_ref, o_ref, kb_s, vt_s, m_s, l_s, acc_s,
                     *, tq, nblk, nh, lam_init):
    i = pl.program_id(2)
    hl = [slice(h * VD_B, (h + 1) * VD_B) for h in range(nh)]

    @pl.when(i == 0)
    def _():
        for jj in range(nblk):
            kb_s[jj] = k_ref[jj * tq:(jj + 1) * tq, :].astype(BF16)
            for h in range(nh):
                vt_s[h, jj] = v_ref[jj * tq:(jj + 1) * tq, hl[h]].T.astype(BF16)

    lane = lax.broadcasted_iota(jnp.int32, (1, LANES), 1)
    m0 = lane < HD_B
    qsts = []
    for h in range(nh):
        q = q_ref[:, hl[h]] * (HD_B ** -0.5)
        qs = jnp.concatenate([jnp.where(m0, q, 0.0), jnp.where(m0, 0.0, q)], axis=0)
        qsts.append(qs.T.astype(BF16))
    m_s[...] = jnp.full_like(m_s, -jnp.inf)
    l_s[...] = jnp.zeros_like(l_s)
    acc_s[...] = jnp.zeros_like(acc_s)

    def block(j, masked):
        kb = kb_s[j]
        sts = [jnp.dot(kb[:, hl[h]], qsts[h], preferred_element_type=F32) for h in range(nh)]
        if masked:
            rr = lax.broadcasted_iota(jnp.int32, (tq, 2 * tq), 0)
            cc = lax.broadcasted_iota(jnp.int32, (tq, 2 * tq), 1)
            cc = jnp.where(cc >= tq, cc - tq, cc)
            sts = [jnp.where(rr <= cc, st, -jnp.inf) for st in sts]
        alphas, pbs = [], []
        for h in range(nh):
            m_old = m_s[h]
            m_new = jnp.maximum(m_old, jnp.max(sts[h], axis=0, keepdims=True))
            alpha = jnp.exp(m_old - m_new)
            p = jnp.exp(sts[h] - m_new)
            l_s[h] = alpha * l_s[h] + jnp.sum(p, axis=0, keepdims=True)
            m_s[h] = m_new
            alphas.append(alpha)
            pbs.append(p.astype(BF16))
        for h in range(nh):
            vt = vt_s[h, j]
            for c in range(2):
                cs = slice(c * tq, (c + 1) * tq)
                acc_s[h, c] = alphas[h][:, cs] * acc_s[h, c] + jnp.dot(vt, pbs[h][:, cs],
                                                                      preferred_element_type=F32)

    def body(j, carry):
        block(j, False)
        return carry

    lax.fori_loop(0, i, body, 0)
    block(i, True)

    lam = _lambda(lq_ref, lam_init)
    for h in range(nh):
        linv = 1.0 / l_s[h]
        ot = acc_s[h, 0] * linv[:, 0:tq] - lam * (acc_s[h, 1] * linv[:, tq:2 * tq])
        o = _rms_rows(ot.T, sub_ref[...], SUBLN_EPS) * (1.0 - lam_init)
        o_ref[:, hl[h]] = o * _silu(gb_ref[:, hl[h]])


def _diffattn_prompt(p3, k4, v4, l, W, lam_init):
    b, t, _ = p3.shape
    tq = 256
    assert t % tq == 0
    lq = jnp.stack([W['diff_lq1'][l], W['diff_lk1'][l], W['diff_lq2'][l], W['diff_lk2'][l]])
    nh = DIFF_HEADS_PER_STEP
    hw = nh * VD_B
    qb0 = BLK_Q * 1024 // hw
    gb0 = BLK_GB * 1024 // hw
    return pl.pallas_call(
        functools.partial(_diffattn_kernel, tq=tq, nblk=t // tq, nh=nh, lam_init=lam_init),
        grid=(b, H_B // nh, t // tq),
        in_specs=[pl.BlockSpec((None, tq, hw), lambda bi, h, i: (bi, i, qb0 + h)),
                  pl.BlockSpec((None, None, t, hw), lambda bi, h, i: (l, bi, 0, h)),
                  pl.BlockSpec((None, None, t, hw), lambda bi, h, i: (l, bi, 0, h)),
                  pl.BlockSpec((None, tq, hw), lambda bi, h, i: (bi, i, gb0 + h)),
                  pl.BlockSpec((4, HD_B), lambda bi, h, i: (0, 0)),
                  pl.BlockSpec((1, VD_B), lambda bi, h, i: (0, 0))],
        out_specs=pl.BlockSpec((None, tq, hw), lambda bi, h, i: (bi, i, h)),
        out_shape=jax.ShapeDtypeStruct((b, t, H_B * VD_B), F32),
        scratch_shapes=[pltpu.VMEM((t // tq, tq, hw), BF16), pltpu.VMEM((nh, t // tq, VD_B, tq), BF16),
                        pltpu.VMEM((nh, 1, 2 * tq), F32), pltpu.VMEM((nh, 1, 2 * tq), F32),
                        pltpu.VMEM((nh, 2, VD_B, tq), F32)],
        compiler_params=_cparams(("arbitrary", "arbitrary", "arbitrary")), name="diffattn_prompt",
    )(p3, k4, v4, p3, lq, W['diff_subln'][l].reshape(1, VD_B))


def _paged_kernel(pt_ref, q_ref, kn_ref, vn_ref, gb_ref, lq_ref, sub_ref, *rest, pps, lam_init):
    kpages = rest[0:pps]
    vpages = rest[pps:2 * pps]
    o_ref = rest[2 * pps]
    m_s, l_s, acc_s = rest[2 * pps + 1:]
    step = pl.program_id(1)
    nrow = 2 * H_B
    lane = lax.broadcasted_iota(jnp.int32, (1, LANES), 1)
    m0 = lane < HD_B
    q8 = q_ref[...] * (HD_B ** -0.5)
    q16 = jnp.concatenate([jnp.where(m0, q8, 0.0), jnp.where(m0, 0.0, q8)], axis=0)
    npg = PAGE_SIZE * H_B
    rr = lax.broadcasted_iota(jnp.int32, (nrow, npg), 0)
    cc = lax.broadcasted_iota(jnp.int32, (nrow, npg), 1)
    valid = (cc % H_B) == (rr % H_B)

    @pl.when(step == 0)
    def _():
        m_s[...] = jnp.full_like(m_s, -jnp.inf)
        l_s[...] = jnp.zeros_like(l_s)
        acc_s[...] = jnp.zeros_like(acc_s)

    qb = q16.astype(BF16)
    parts = []
    for kp in kpages:
        kf = kp[...].reshape(npg, VD_B).astype(BF16)
        sc = lax.dot_general(qb, kf, (((1,), (1,)), ((), ())), preferred_element_type=F32)
        parts.append(jnp.where(valid, sc, -jnp.inf))
    s = jnp.concatenate(parts, axis=-1)
    m_old = m_s[...]
    m_new = jnp.maximum(m_old, jnp.max(s, axis=-1, keepdims=True))
    alpha = jnp.exp(m_old - m_new)
    p = jnp.exp(s - m_new)
    l_s[...] = alpha * l_s[...] + jnp.sum(p, axis=-1, keepdims=True)
    m_s[...] = m_new
    pb = p.astype(BF16)
    acc = alpha * acc_s[...]
    for u, vp in enumerate(vpages):
        vf = vp[...].reshape(npg, VD_B).astype(BF16)
        acc = acc + jnp.dot(pb[:, u * npg:(u + 1) * npg], vf, preferred_element_type=F32)
    acc_s[...] = acc

    @pl.when(step == pl.num_programs(1) - 1)
    def _():
        kn = kn_ref[...]
        vn = vn_ref[...]
        s_new = jnp.sum(q16 * jnp.concatenate([kn, kn], axis=0), axis=-1, keepdims=True)
        m_o = m_s[...]
        m_f = jnp.maximum(m_o, s_new)
        a_f = jnp.exp(m_o - m_f)
        p_new = jnp.exp(s_new - m_f)
        l_f = a_f * l_s[...] + p_new
        acc_f = a_f * acc_s[...] + p_new * jnp.concatenate([vn, vn], axis=0)
        o16 = acc_f / l_f
        o = o16[0:H_B] - _lambda(lq_ref, lam_init) * o16[H_B:2 * H_B]
        o = _rms_rows(o, sub_ref[...], SUBLN_EPS) * (1.0 - lam_init)
        o_ref[...] = o * _silu(gb_ref[...])


def _paged_attn(q3, kn3, vn3, gb3, cache_k, cache_v, page_table, l, W, lam_init):
    b = q3.shape[0]
    n_pages = page_table.shape[1]
    pps = PAGES_PER_STEP
    while n_pages % pps:
        pps //= 2
    lq = jnp.stack([W['diff_lq1'][l], W['diff_lk1'][l], W['diff_lq2'][l], W['diff_lk2'][l]])
    tok = pl.BlockSpec((None, H_B, VD_B), lambda bi, s, pt: (bi, 0, 0))

    def page_spec(u):
        return pl.BlockSpec((None, None, PAGE_SIZE, H_B, VD_B),
                            lambda bi, s, pt: (l, pt[bi, s * pps + u], 0, 0, 0))

    grid_spec = pltpu.PrefetchScalarGridSpec(
        num_scalar_prefetch=1, grid=(b, n_pages // pps),
        in_specs=[tok, tok, tok, tok,
                  pl.BlockSpec((4, HD_B), lambda bi, s, pt: (0, 0)),
                  pl.BlockSpec((1, VD_B), lambda bi, s, pt: (0, 0))]
                 + [page_spec(u) for u in range(pps)] + [page_spec(u) for u in range(pps)],
        out_specs=tok,
        scratch_shapes=[pltpu.VMEM((2 * H_B, 1), F32), pltpu.VMEM((2 * H_B, 1), F32),
                        pltpu.VMEM((2 * H_B, VD_B), F32)])
    return pl.pallas_call(
        functools.partial(_paged_kernel, pps=pps, lam_init=lam_init),
        grid_spec=grid_spec, out_shape=jax.ShapeDtypeStruct((b, H_B, VD_B), F32),
        compiler_params=_cparams(("arbitrary", "arbitrary")), name="paged_diffattn",
    )(page_table, q3, kn3, vn3, gb3, lq, W['diff_subln'][l].reshape(1, VD_B),
      *([cache_k] * pps), *([cache_v] * pps))


def _lru_gates(xconv, wa_ref, ba, wx_ref, bx, lam, dotf):
    pr, pi = [], []
    for n in range(NB_C):
        xb = xconv[:, n * BS_C:(n + 1) * BS_C]
        pr.append(dotf(xb, wa_ref[n]))
        pi.append(dotf(xb, wx_ref[n]))
    gate_r = _sigmoid(jnp.concatenate(pr, axis=-1) + ba)
    gate_i = _sigmoid(jnp.concatenate(pi, axis=-1) + bx)
    log_a = -LRU_C * gate_r * _softplus(-lam)
    a = jnp.exp(log_a)
    bv = jnp.sqrt(1.0 - jnp.exp(2.0 * log_a)) * (gate_i * xconv)
    return a, bv


def _lru_prompt_kernel(xc_ref, gc_ref, cp_ref, h0_ref, cw_ref, cb_ref, wa_ref, ba_ref, wx_ref, bx_ref, lam_ref,
                       oc_ref, h_ref, ext_s, hc_s, *, tt):
    t = pl.program_id(1)

    @pl.when(t == 0)
    def _():
        ext_s[0:SUBLANES, :] = cp_ref[...]
        hc_s[...] = h0_ref[...]

    xc = xc_ref[...]
    ext_s[SUBLANES:SUBLANES + tt, :] = xc
    cw = cw_ref[...]
    xconv = cw[3:4] * xc + cb_ref[...]
    for j in range(1, CONV_W):
        xconv = xconv + cw[3 - j:4 - j] * ext_s[pl.ds(SUBLANES - j, tt), :]
    ext_s[0:SUBLANES, :] = xc[tt - SUBLANES:tt, :]

    a, bv = _lru_gates(xconv, wa_ref, ba_ref[...], wx_ref, bx_ref[...], lam_ref[...], _dot1)
    rowi = lax.broadcasted_iota(jnp.int32, (tt, 1), 0) % SUBLANES
    s = 1
    while s < SUBLANES:
        keep = rowi >= s
        a_sh = jnp.where(keep, pltpu.roll(a, s, 0), 1.0)
        b_sh = jnp.where(keep, pltpu.roll(bv, s, 0), 0.0)
        bv = a * b_sh + bv
        a = a * a_sh
        s *= 2
    hprev = hc_s[...]
    hs = []
    for gi in range(tt // SUBLANES):
        rows = slice(gi * SUBLANES, (gi + 1) * SUBLANES)
        hg = a[rows] * hprev + bv[rows]
        hs.append(hg)
        hprev = hg[SUBLANES - 1:SUBLANES, :]
    h = jnp.concatenate(hs, axis=0)
    hc_s[...] = hprev
    h_ref[...] = hprev
    oc_ref[...] = h * _silu(gc_ref[...])


def _lru_prompt(p3, l, W, conv_prev8, h0):
    b, t, _ = p3.shape
    tt = 256
    assert t % tt == 0
    row = lambda a: a.reshape(1, -1)
    const = lambda shape: pl.BlockSpec(shape, lambda i, j: (0,) * len(shape))
    oc, h = pl.pallas_call(
        functools.partial(_lru_prompt_kernel, tt=tt),
        grid=(b, t // tt),
        in_specs=[pl.BlockSpec((None, tt, D_C), lambda i, j: (i, j, BLK_XC)),
                  pl.BlockSpec((None, tt, D_C), lambda i, j: (i, j, BLK_GC)),
                  pl.BlockSpec((None, SUBLANES, D_C), lambda i, j: (i, 0, 0)),
                  pl.BlockSpec((None, 1, D_C), lambda i, j: (i, 0, 0)),
                  const((CONV_W, D_C)), const((1, D_C)), const((NB_C, BS_C, BS_C)), const((1, D_C)),
                  const((NB_C, BS_C, BS_C)), const((1, D_C)), const((1, D_C))],
        out_specs=[pl.BlockSpec((None, tt, D_C), lambda i, j: (i, j, 0)),
                   pl.BlockSpec((None, 1, D_C), lambda i, j: (i, 0, 0))],
        out_shape=[jax.ShapeDtypeStruct((b, t, D_C), F32), jax.ShapeDtypeStruct((b, 1, D_C), F32)],
        scratch_shapes=[pltpu.VMEM((tt + SUBLANES, D_C), F32), pltpu.VMEM((1, D_C), F32)],
        compiler_params=_cparams(("arbitrary", "arbitrary")), name="lru_prompt",
    )(p3, p3, conv_prev8, h0, W['lru_conv_w'][l], row(W['lru_conv_b'][l]), W['wa_bf16'][l], row(W['lru_ba'][l]),
      W['wx_bf16'][l], row(W['lru_bx'][l]), row(W['lru_lambda'][l]))
    return oc, h.reshape(b, D_C)


def _lru_step_kernel(xc_ref, gc_ref, c0_ref, c1_ref, c2_ref, h0_ref, cw_ref, cb_ref, wa_ref, ba_ref, wx_ref,
                     bx_ref, lam_ref, oc_ref, h_ref):
    cw = cw_ref[...]
    xconv = (cw[0:1] * c0_ref[...] + cw[1:2] * c1_ref[...] + cw[2:3] * c2_ref[...] + cw[3:4] * xc_ref[...]
             + cb_ref[...])
    a, bv = _lru_gates(xconv, wa_ref, ba_ref[...], wx_ref, bx_ref[...], lam_ref[...], _dot3)
    h = a * h0_ref[...] + bv
    h_ref[...] = h
    oc_ref[...] = h * _silu(gc_ref[...])


def _lru_step(ps2, l, W, conv_prev, h0):
    b = ps2.shape[0]
    row = lambda a: a.reshape(1, -1)
    full = lambda shape: pl.BlockSpec(shape, lambda i: (0,) * len(shape))
    xc = ps2[:, BLK_XC * 1024:(BLK_XC + 1) * 1024]
    gc = ps2[:, BLK_GC * 1024:(BLK_GC + 1) * 1024]
    vec = jax.ShapeDtypeStruct((b, D_C), F32)
    oc, h = pl.pallas_call(
        _lru_step_kernel, grid=(1,),
        in_specs=[full((b, D_C))] * 6 + [full((CONV_W, D_C)), full((1, D_C)), full((NB_C, BS_C, BS_C)),
                                          full((1, D_C)), full((NB_C, BS_C, BS_C)), full((1, D_C)), full((1, D_C))],
        out_specs=[full((b, D_C))] * 2, out_shape=[vec, vec],
        compiler_params=_cparams(("arbitrary",)), name="lru_step",
    )(xc, gc, conv_prev[:, 0], conv_prev[:, 1], conv_prev[:, 2], h0, W['lru_conv_w'][l], row(W['lru_conv_b'][l]),
      W['lru_wa'][l], row(W['lru_ba'][l]), W['lru_wx'][l], row(W['lru_bx'][l]), row(W['lru_lambda'][l]))
    conv_new = jnp.stack([conv_prev[:, 1], conv_prev[:, 2], xc], axis=1)
    return oc, h, conv_new


def _merge_kernel(x_ref, oa_ref, ob_ref, oc_ref, g_ref, wb_ref, wo_ref, nm_ref, wq_ref, x1_ref, q_ref):
    acc = None
    for n, o_ref in enumerate((oa_ref, ob_ref, oc_ref)):
        tproj = jnp.dot(o_ref[...].astype(BF16), wb_ref[n], preferred_element_type=F32)
        term = _sigmoid(g_ref[:, n * D_MODEL:(n + 1) * D_MODEL]) * tproj
        acc = term if acc is None else acc + term
    x1 = x_ref[...] + jnp.dot(acc.astype(BF16), wo_ref[...], preferred_element_type=F32)
    x1_ref[...] = x1
    xn = _rms_rows(x1, nm_ref[...], RMS_EPS)
    q_ref[...] = jnp.dot(xn.astype(BF16), wq_ref[...], preferred_element_type=F32)


def _merge(x2, oa2, ob2, oc2, p2, l, W, tm):
    m = x2.shape[0]
    assert m % tm == 0
    tok = pl.BlockSpec((tm, D_MODEL), lambda i: (i, 0))
    const = lambda shape: pl.BlockSpec(shape, lambda i: (0,) * len(shape))
    out = jax.ShapeDtypeStruct((m, D_MODEL), F32)
    return pl.pallas_call(
        _merge_kernel, grid=(m // tm,),
        in_specs=[tok, tok, tok, tok, pl.BlockSpec((tm, 3 * D_MODEL), lambda i: (i, BLK_GTS // 3)),
                  const((3, D_MODEL, D_MODEL)), const((D_MODEL, D_MODEL)), const((1, D_MODEL)),
                  const((D_MODEL, D_MODEL))],
        out_specs=[tok, tok], out_shape=[out, out],
        compiler_params=_cparams(("arbitrary",)), name="merge_outproj",
    )(x2, oa2, ob2, oc2, p2, W['wb_bf16'][l], W['wout_bf16'][l], W['norm_mem'][l].reshape(1, D_MODEL),
      W['wq_bf16'][l])


def _memattn_kernel(q_ref, mk_ref, mv_ref, o_ref):
    outs = []
    for h in range(H_MEM):
        sl = slice(h * HD_MEM, (h + 1) * HD_MEM)
        s = lax.dot_general(q_ref[:, sl].astype(BF16), mk_ref[:, sl].astype(BF16), (((1,), (1,)), ((), ())),
                            preferred_element_type=F32) * (HD_MEM ** -0.5)
        m = jnp.max(s, axis=-1, keepdims=True)
        p = jnp.exp(s - m)
        p = p / jnp.sum(p, axis=-1, keepdims=True)
        outs.append(jnp.dot(p.astype(BF16), mv_ref[:, sl].astype(BF16), preferred_element_type=F32))
    o_ref[...] = jnp.concatenate(outs, axis=-1)


def _memattn(q3, mk3, mv3, tq):
    b, t, _ = q3.shape
    assert t % tq == 0
    tok = pl.BlockSpec((None, tq, D_MODEL), lambda i, j: (i, j, 0))
    mem = pl.BlockSpec((None, N_MEM, D_MODEL), lambda i, j: (i, 0, 0))
    return pl.pallas_call(
        _memattn_kernel, grid=(b, t // tq), in_specs=[tok, mem, mem], out_specs=tok,
        out_shape=jax.ShapeDtypeStruct((b, t, D_MODEL), F32),
        compiler_params=_cparams(("arbitrary", "arbitrary")), name="mem_attn",
    )(q3, mk3, mv3)


def _oproj_kernel(x_ref, o_ref, w_ref, g_ref, y_ref, *, final):
    x2 = x_ref[...] + jnp.dot(o_ref[...].astype(BF16), w_ref[...], preferred_element_type=F32)
    y_ref[...] = _rms_rows(x2, g_ref[...], RMS_EPS) if final else x2


def _oproj(x2, o2, w_bf16, g, final, tm):
    m = x2.shape[0]
    assert m % tm == 0
    tok = pl.BlockSpec((tm, D_MODEL), lambda i: (i, 0))
    return pl.pallas_call(
        functools.partial(_oproj_kernel, final=final), grid=(m // tm,),
        in_specs=[tok, tok, pl.BlockSpec((D_MODEL, D_MODEL), lambda i: (0, 0)),
                  pl.BlockSpec((1, D_MODEL), lambda i: (0, 0))],
        out_specs=tok, out_shape=jax.ShapeDtypeStruct((m, D_MODEL), F32),
        compiler_params=_cparams(("arbitrary",)), name="mem_outproj",
    )(x2, o2, w_bf16, g.reshape(1, D_MODEL))


def _prep_weights(W):
    w_in = W['w_in']
    depth = w_in.shape[0]
    offs = [0]
    for s in (W_SHIFT, D_A, 1024, 1024, D_MODEL, D_MODEL, D_C, D_C, 3 * D_MODEL):
        offs.append(offs[-1] + s)
    seg = lambda n: w_in[:, :, offs[n]:offs[n + 1]]
    pa, ga, qb, kb, vb, gb, xc, gc, gts = (seg(n) for n in range(9))
    pad = jnp.zeros((depth, D_MODEL, PA_PAD - W_SHIFT), w_in.dtype)
    wp = jnp.concatenate([pa, pad, ga, qb, gb, xc, gc, gts], axis=-1)
    wkv = jnp.concatenate([kb, vb], axis=-1)
    z = jnp.zeros((depth, R_W, D_A), F32)
    out = dict(W)
    out['wp'], out['wkv'] = wp, wkv
    out['wp_bf16'], out['wkv_bf16'] = wp.astype(BF16), wkv.astype(BF16)
    out['w2p'] = jnp.concatenate([W['rwkv_w2'], z], axis=1)
    out['a2p'] = jnp.concatenate([z, W['rwkv_a2']], axis=1)
    out['w2p_bf16'], out['a2p_bf16'] = out['w2p'].astype(BF16), out['a2p'].astype(BF16)
    out['wa_bf16'], out['wx_bf16'] = W['lru_wa'].astype(BF16), W['lru_wx'].astype(BF16)
    out['wb_bf16'] = W['w_branch'].astype(BF16)
    out['wout_bf16'] = W['w_out'].astype(BF16)
    out['wq_bf16'] = W['w_mem_q'].astype(BF16)
    out['wmkv_bf16'] = W['w_mem_kv'].astype(BF16)
    out['wmo_bf16'] = W['w_mem_o'].astype(BF16)
    return out


def _prompt_layer(W, l, x3, mem2, last, kv_prev=None):
    b, t, _ = x3.shape
    depth = W['w_in'].shape[0]
    x2 = x3.reshape(b * t, D_MODEL)
    lam_init = 0.8 - 0.6 * math.exp(-0.3 * l)
    mkv = _rmsmm(mem2, W['norm_memkv'][l], W['wmkv_bf16'][l], tm=min(1024, mem2.shape[0]), tn=D_MODEL, nsplit=2)
    tm = min(1024, b * t)
    p2 = _rmsmm(x2, W['norm_mix'][l], W['wp_bf16'][l], tm=tm, tn=2048)
    k_all, v_all = _kvproj(x2, W['norm_mix'][l], W['wkv_bf16'][l], l, depth, kv_prev, tm=tm)
    p3 = p2.reshape(b, t, N_PROJ)
    k4 = k_all.reshape(depth, b, t, D_MODEL)
    v4 = v_all.reshape(depth, b, t, D_MODEL)
    oa, s_t = _rwkv_prompt(p3, l, W)
    ob = _diffattn_prompt(p3, k4, v4, l, W, lam_init)
    oc, h_t = _lru_prompt(p3, l, W, jnp.zeros((b, SUBLANES, D_C), F32), jnp.zeros((b, 1, D_C), F32))
    tmm = min(256, b * t)
    x1, q = _merge(x2, oa.reshape(b * t, D_A), ob.reshape(b * t, D_MODEL), oc.reshape(b * t, D_C), p2, l, W, tmm)
    mk3 = mkv[0].reshape(b, N_MEM, D_MODEL)
    mv3 = mkv[1].reshape(b, N_MEM, D_MODEL)
    o = _memattn(q.reshape(b, t, D_MODEL), mk3, mv3, min(512, t))
    g = W['norm_final'] if last else W['norm_mem'][l]
    xo = _oproj(x1, o.reshape(b * t, D_MODEL), W['wmo_bf16'][l], g, last, min(512, b * t))
    mk_l = mk3.reshape(b, N_MEM, H_MEM, HD_MEM)
    mv_l = mv3.reshape(b, N_MEM, H_MEM, HD_MEM)
    shift = p3[:, t - 1, 0:W_SHIFT]
    conv = p3[:, t - (CONV_W - 1):, BLK_XC * 1024:(BLK_XC + 1) * 1024]
    return xo.reshape(b, t, D_MODEL), (k_all, v_all), (mk_l, mv_l, s_t, shift, h_t, conv)


def _sample_layer(W, l, xs3, shift_prev, s0, conv_prev, h0, mem_k, mem_v, cache_k4, cache_v4, page_table, last):
    b = xs3.shape[0]
    x2 = xs3.reshape(b, D_MODEL)
    lam_init = 0.8 - 0.6 * math.exp(-0.3 * l)
    p2 = _rmsmm(x2, W['norm_mix'][l], W['wp'][l], tm=b, tn=1024, precise=True)
    kv = _rmsmm(x2, W['norm_mix'][l], W['wkv'][l], tm=b, tn=D_MODEL, nsplit=2, precise=True)
    oa, s_new = _rwkv_step(p2, l, W, shift_prev, s0)
    q3 = p2[:, BLK_Q * 1024:(BLK_Q + 1) * 1024].reshape(b, H_B, VD_B)
    gb3 = p2[:, BLK_GB * 1024:(BLK_GB + 1) * 1024].reshape(b, H_B, VD_B)
    ob = _paged_attn(q3, kv[0].reshape(b, H_B, VD_B), kv[1].reshape(b, H_B, VD_B), gb3, cache_k4, cache_v4,
                     page_table, l, W, lam_init)
    oc, h_new, conv_new = _lru_step(p2, l, W, conv_prev, h0)
    x1, q = _merge(x2, oa, ob.reshape(b, D_MODEL), oc, p2, l, W, b)
    o = _memattn(q.reshape(b, 1, D_MODEL), mem_k.reshape(b, N_MEM, D_MODEL), mem_v.reshape(b, N_MEM, D_MODEL), 1)
    g = W['norm_final'] if last else W['norm_mem'][l]
    xo = _oproj(x1, o.reshape(b, D_MODEL), W['wmo_bf16'][l], g, last, b)
    k_l = kv[0].reshape(b, 1, H_B, 2 * HD_B)
    v_l = kv[1].reshape(b, 1, H_B, VD_B)
    shift = p2[:, 0:W_SHIFT]
    return xo.reshape(b, 1, D_MODEL), (k_l, v_l, s_new, shift, h_new, conv_new)


def kernel(x_prompt, x_sample, cache_diff_k, cache_diff_v, cache_mem_k, cache_mem_v, state_rwkv_S, state_rwkv_shift, state_lru_h, state_lru_conv, page_table, mem_prompt, norm_mix, w_in, rwkv_mu, rwkv_w0, rwkv_w2, rwkv_a0, rwkv_a2, rwkv_kk, rwkv_ka, rwkv_rk, rwkv_ln_g, rwkv_ln_b, diff_lq1, diff_lk1, diff_lq2, diff_lk2, diff_subln, lru_conv_w, lru_conv_b, lru_wa, lru_ba, lru_wx, lru_bx, lru_lambda, w_branch, w_out, norm_mem, norm_memkv, w_mem_q, w_mem_kv, w_mem_o, norm_final):
    W = _prep_weights(dict(
        norm_mix=norm_mix, w_in=w_in, rwkv_mu=rwkv_mu, rwkv_w0=rwkv_w0, rwkv_w2=rwkv_w2, rwkv_a0=rwkv_a0,
        rwkv_a2=rwkv_a2, rwkv_kk=rwkv_kk, rwkv_ka=rwkv_ka, rwkv_rk=rwkv_rk, rwkv_ln_g=rwkv_ln_g,
        rwkv_ln_b=rwkv_ln_b, diff_lq1=diff_lq1, diff_lk1=diff_lk1, diff_lq2=diff_lq2, diff_lk2=diff_lk2,
        diff_subln=diff_subln, lru_conv_w=lru_conv_w, lru_conv_b=lru_conv_b, lru_wa=lru_wa, lru_ba=lru_ba,
        lru_wx=lru_wx, lru_bx=lru_bx, lru_lambda=lru_lambda, w_branch=w_branch, w_out=w_out, norm_mem=norm_mem,
        norm_memkv=norm_memkv, w_mem_q=w_mem_q, w_mem_kv=w_mem_kv, w_mem_o=w_mem_o, norm_final=norm_final))
    depth = w_in.shape[0]
    bp = x_prompt.shape[0]
    mem2 = mem_prompt.reshape(bp * N_MEM, D_MODEL)

    xp = x_prompt
    tp = x_prompt.shape[1]
    p_outs = []
    kv_bufs = None
    for l in range(depth):
        xp, kv_bufs, outs = _prompt_layer(W, l, xp, mem2, l == depth - 1, kv_bufs)
        p_outs.append(outs)
    p_k = kv_bufs[0].reshape(depth, bp, tp, H_B, 2 * HD_B)
    p_v = kv_bufs[1].reshape(depth, bp, tp, H_B, VD_B)

    ck4, cv4 = cache_diff_k, cache_diff_v
    xs = x_sample
    s_outs = []
    for l in range(depth):
        xs, outs = _sample_layer(W, l, xs, state_rwkv_shift[l], state_rwkv_S[l], state_lru_conv[l], state_lru_h[l],
                                 cache_mem_k[l], cache_mem_v[l], ck4, cv4, page_table, l == depth - 1)
        s_outs.append(outs)

    stack = lambda outs, n: jnp.stack([o[n] for o in outs])
    return (xp, xs,
            p_k, p_v, stack(p_outs, 0), stack(p_outs, 1),
            stack(p_outs, 2), stack(p_outs, 3), stack(p_outs, 4), stack(p_outs, 5),
            stack(s_outs, 0), stack(s_outs, 1), stack(s_outs, 2), stack(s_outs, 3),
            stack(s_outs, 4), stack(s_outs, 5))
```
